```python
import math
import jax
import jax.numpy as jnp
from jax import lax
import numpy as np


D_MODEL = 1024
BATCH = 1
SEQ = 16384
DEPTH = 2

GRID_W = 64
CTX_LEN = 256
EPS = 1e-6
NEG = -1e30
Q_BLOCK = 128
ROPE_DIM = 64
ROPE_THETA = 10000.0

A_HEADS = 8
A_KV_HEADS = 2
A_HEAD_DIM = ROPE_DIM
A_WINDOW = 128
B_HEADS = 4
B_HEAD_DIM = ROPE_DIM
C_HEADS = 4
C_Q_RANK = 256
C_KV_RANK = 128
C_NOPE_DIM = 128
C_ROPE_DIM = ROPE_DIM
C_V_DIM = 128
D_HEADS = 8
D_HEAD_DIM = 64
D_WIN_ROWS = 8
D_WIN_COLS = 16

MIX_EVEN = A_HEADS * A_HEAD_DIM + B_HEADS * 2 * B_HEAD_DIM
MIX_ODD = C_HEADS * C_V_DIM + D_HEADS * D_HEAD_DIM
EVEN_SPLITS = (A_HEADS * A_HEAD_DIM, A_KV_HEADS * A_HEAD_DIM, A_KV_HEADS * A_HEAD_DIM, B_HEADS * 2 * B_HEAD_DIM, B_HEADS * 2 * B_HEAD_DIM, B_HEADS * 2 * B_HEAD_DIM, MIX_EVEN)
ODD_SPLITS = (C_Q_RANK, C_KV_RANK, C_ROPE_DIM, D_HEADS * D_HEAD_DIM, D_HEADS * D_HEAD_DIM, D_HEADS * D_HEAD_DIM, MIX_ODD)
IN_EVEN = sum(EVEN_SPLITS)
IN_ODD = sum(ODD_SPLITS)

kernel_name = 'hybrid_dit_window_diff_mla_natten'


def rmsnorm(x, g):
    xf = x.astype(jnp.float32)
    y = xf * lax.rsqrt(jnp.mean(xf * xf, axis=-1, keepdims=True) + EPS)
    return (y * g.astype(jnp.float32)).astype(x.dtype)


def _split(t, sizes):
    return jnp.split(t, np.cumsum(sizes)[:-1].tolist(), axis=-1)


def _axial_rope_tables(S, dim):
    t = jnp.arange(S)
    row = (t // GRID_W).astype(jnp.float32)
    col = (t % GRID_W).astype(jnp.float32)
    quarter = dim // 4
    inv = ROPE_THETA ** (-jnp.arange(quarter, dtype=jnp.float32) / quarter)
    ang_r = row[:, None] * inv[None, :]
    ang_c = col[:, None] * inv[None, :]
    ang = jnp.concatenate([ang_r, ang_r, ang_c, ang_c], axis=-1)
    return jnp.cos(ang), jnp.sin(ang)


def _rope(x, cos, sin):
    half = x.shape[-1] // 2
    qtr = half // 2
    xr, xc = x[..., :half], x[..., half:]
    rot = jnp.concatenate([-xr[..., qtr:], xr[..., :qtr], -xc[..., qtr:], xc[..., :qtr]], axis=-1)
    return (x * cos[:, None, :] + rot * sin[:, None, :]).astype(x.dtype)


def _modulate(x, cond, norm_g, w_ada, b_ada):
    mod = jax.nn.silu(cond) @ w_ada + b_ada
    shift, scale, gate = jnp.split(mod[:, None, :], 3, axis=-1)
    return rmsnorm(x, norm_g) * (1.0 + scale) + shift, gate


def _sweep_query_blocks(fn, q):
    B, S = q.shape[:2]
    nb = S // Q_BLOCK
    qb = jnp.moveaxis(q.reshape((B, nb, Q_BLOCK) + q.shape[2:]), 1, 0)
    out = lax.map(lambda a: fn(a[0], a[1]), (jnp.arange(nb), qb))
    out = jnp.moveaxis(out, 0, 1)
    return out.reshape((B, S) + out.shape[3:])


def _softmax_attend(q, k, v):
    s = jnp.einsum('bqhd,bkhd->bhqk', q, k).astype(jnp.float32) * (q.shape[-1] ** -0.5)
    p = jax.nn.softmax(s, axis=-1).astype(v.dtype)
    return jnp.einsum('bhqk,bkhe->bqhe', p, v)


def _sink_probs(parts, sink):
    base = parts[0]
    Hk, G = base.shape[1], base.shape[2]
    snk = jnp.broadcast_to(sink.astype(jnp.float32).reshape(Hk, G)[None, :, :, None, None], base.shape[:-1] + (1,))
    p = jax.nn.softmax(jnp.concatenate(list(parts) + [snk], axis=-1), axis=-1)
    return p[..., :-1]


def _window_gqa_sink(q, k, v, kc, vc, sink):
    B, S, Hk, G, d = q.shape
    L = kc.shape[1]
    W = A_WINDOW
    span = Q_BLOCK + 2 * W
    scale = d ** -0.5
    kp = jnp.pad(k, ((0, 0), (W, W), (0, 0), (0, 0)))
    vp = jnp.pad(v, ((0, 0), (W, W), (0, 0), (0, 0)))

    def block(n, qn):
        start = n * Q_BLOCK
        kn = lax.dynamic_slice_in_dim(kp, start, span, axis=1)
        vn = lax.dynamic_slice_in_dim(vp, start, span, axis=1)
        qpos = start + jnp.arange(Q_BLOCK)
        kpos = start - W + jnp.arange(span)
        valid = (jnp.abs(qpos[:, None] - kpos[None, :]) <= W) & (kpos >= 0)[None, :] & (kpos < S)[None, :]
        s_loc = jnp.einsum('bqkgd,bskd->bkgqs', qn, kn).astype(jnp.float32) * scale
        s_loc = jnp.where(valid, s_loc, NEG)
        s_ctx = jnp.einsum('bqkgd,bckd->bkgqc', qn, kc).astype(jnp.float32) * scale
        p = _sink_probs([s_loc, s_ctx], sink).astype(v.dtype)
        return (jnp.einsum('bkgqs,bskd->bqkgd', p[..., :span], vn)
                + jnp.einsum('bkgqc,bckd->bqkgd', p[..., span:span + L], vc))

    return _sweep_query_blocks(block, q)


def _ctx_sink_attend(q, k, v, sink):
    s = jnp.einsum('bqkgd,bckd->bkgqc', q, k).astype(jnp.float32) * (q.shape[-1] ** -0.5)
    p = _sink_probs([s], sink).astype(v.dtype)
    return jnp.einsum('bkgqc,bckd->bqkgd', p, v)


def _diff_lambda(b_lambda, lam_init):
    lf = b_lambda.astype(jnp.float32)
    return jnp.exp(jnp.sum(lf[0] * lf[1])) - jnp.exp(jnp.sum(lf[2] * lf[3])) + lam_init


def _diff_attend(q, k, v, lam):
    s = jnp.einsum('bqhte,bkhte->bhtqk', q, k).astype(jnp.float32) * (q.shape[-1] ** -0.5)
    p = jax.nn.softmax(s, axis=-1)
    a = (p[:, :, 0] - lam * p[:, :, 1]).astype(v.dtype)
    return jnp.einsum('bhqk,bkhe->bqhe', a, v)


def _mla_qkv(cq, ckv, kr, q_norm_g, kv_norm_g, w_qb, w_kvb, cos, sin):
    B, T, _ = cq.shape
    q = (rmsnorm(cq, q_norm_g) @ w_qb).reshape(B, T, C_HEADS, C_NOPE_DIM + C_ROPE_DIM)
    kv = (rmsnorm(ckv, kv_norm_g) @ w_kvb).reshape(B, T, C_HEADS, C_NOPE_DIM + C_V_DIM)
    q_nope, q_pe = q[..., :C_NOPE_DIM], q[..., C_NOPE_DIM:]
    k_nope, v = kv[..., :C_NOPE_DIM], kv[..., C_NOPE_DIM:]
    k_pe = kr[:, :, None, :]
    if cos is not None:
        q_pe = _rope(q_pe, cos, sin)
        k_pe = _rope(k_pe, cos, sin)
    q = jnp.concatenate([q_nope, q_pe], axis=-1)
    k = jnp.concatenate([k_nope, jnp.broadcast_to(k_pe, (B, T, C_HEADS, C_ROPE_DIM))], axis=-1)
    return q, k, v


def _neighbourhood_attend(q, k, v, kc, vc, rpb):
    B, S, H, d = q.shape
    L = kc.shape[1]
    rows = S // GRID_W
    kr_n = min(D_WIN_ROWS, rows)
    kw_n = D_WIN_COLS
    n_loc = kr_n * GRID_W
    scale = d ** -0.5
    kg = k.reshape(B, rows, GRID_W, H, d)
    vg = v.reshape(B, rows, GRID_W, H, d)
    cols = jnp.arange(GRID_W)
    cs = jnp.clip(cols - kw_n // 2, 0, GRID_W - kw_n)
    col_ok = (cols[None, :] >= cs[:, None]) & (cols[None, :] < cs[:, None] + kw_n)
    mask = jnp.broadcast_to(col_ok[:, None, :], (GRID_W, kr_n, GRID_W)).reshape(GRID_W, n_loc)
    dc = jnp.clip(cols[None, :] - cols[:, None] + (D_WIN_COLS - 1), 0, 2 * D_WIN_COLS - 2)
    rpb_f = rpb.astype(jnp.float32)
    qg = jnp.moveaxis(q.reshape(B, rows, GRID_W, H, d), 1, 0)

    def row_fn(args):
        r, qr = args
        rs = jnp.clip(r - kr_n // 2, 0, rows - kr_n)
        ks = lax.dynamic_slice_in_dim(kg, rs, kr_n, axis=1).reshape(B, n_loc, H, d)
        vs = lax.dynamic_slice_in_dim(vg, rs, kr_n, axis=1).reshape(B, n_loc, H, d)
        dr = rs + jnp.arange(kr_n) - r + (D_WIN_ROWS - 1)
        bias = rpb_f[:, dr[:, None, None], dc[None, :, :]]
        bias = jnp.transpose(bias, (0, 2, 1, 3)).reshape(H, GRID_W, n_loc)
        s_loc = jnp.einsum('bqhd,bkhd->bhqk', qr, ks).astype(jnp.float32) * scale + bias[None]
        s_loc = jnp.where(mask, s_loc, NEG)
        s_ctx = jnp.einsum('bqhd,bchd->bhqc', qr, kc).astype(jnp.float32) * scale
        p = jax.nn.softmax(jnp.concatenate([s_loc, s_ctx], axis=-1), axis=-1).astype(v.dtype)
        return (jnp.einsum('bhqk,bkhd->bqhd', p[..., :n_loc], vs)
                + jnp.einsum('bhqc,bchd->bqhd', p[..., n_loc:], vc))

    out = lax.map(row_fn, (jnp.arange(rows), qg))
    return jnp.moveaxis(out, 0, 1).reshape(B, S, H, d)


def _even_layer(x, xc, c, c_ctx, li, norm_g, w_ada, b_ada, w_in, a_sink, b_lambda, b_subln_g, w_out, cos, sin, ctx_out):
    B, S, _ = x.shape
    L = xc.shape[1]
    d = A_HEAD_DIM
    e = B_HEAD_DIM
    Hk, G = A_KV_HEADS, A_HEADS // A_KV_HEADS
    h, gate = _modulate(x, c, norm_g, w_ada, b_ada)
    hc, gate_c = _modulate(xc, c_ctx[None, :], norm_g, w_ada, b_ada)
    qa, ka, va, qb, kb, vb, g = _split(h @ w_in, EVEN_SPLITS)
    qac, kac, vac, qbc, kbc, vbc, gc = _split(hc @ w_in, EVEN_SPLITS)
    qa = _rope(qa.reshape(B, S, A_HEADS, d), cos, sin).reshape(B, S, Hk, G, d)
    ka = _rope(ka.reshape(B, S, Hk, d), cos, sin)
    va = va.reshape(B, S, Hk, d)
    kac = kac.reshape(B, L, Hk, d)
    vac = vac.reshape(B, L, Hk, d)
    ya = _window_gqa_sink(qa, ka, va, kac, vac, a_sink)
    qb = _rope(qb.reshape(B, S, B_HEADS * 2, e), cos, sin).reshape(B, S, B_HEADS, 2, e)
    kb = _rope(kb.reshape(B, S, B_HEADS * 2, e), cos, sin).reshape(B, S, B_HEADS, 2, e)
    vb = vb.reshape(B, S, B_HEADS, 2 * e)
    kbc = kbc.reshape(B, L, B_HEADS, 2, e)
    vbc = vbc.reshape(B, L, B_HEADS, 2 * e)
    lam_init = 0.8 - 0.6 * math.exp(-0.3 * li)
    lam = _diff_lambda(b_lambda, lam_init)
    kb_all = jnp.concatenate([kb, kbc], axis=1)
    vb_all = jnp.concatenate([vb, vbc], axis=1)
    yb = _sweep_query_blocks(lambda n, qn: _diff_attend(qn, kb_all, vb_all, lam), qb)
    yb = rmsnorm(yb, b_subln_g) * (1.0 - lam_init)
    y = jnp.concatenate([ya.reshape(B, S, -1), yb.reshape(B, S, -1)], axis=-1) * jax.nn.silu(g)
    x = x + gate * (y @ w_out)
    if ctx_out:
        yac = _ctx_sink_attend(qac.reshape(B, L, Hk, G, d), kac, vac, a_sink)
        ybc = rmsnorm(_diff_attend(qbc.reshape(B, L, B_HEADS, 2, e), kbc, vbc, lam), b_subln_g) * (1.0 - lam_init)
        yc = jnp.concatenate([yac.reshape(B, L, -1), ybc.reshape(B, L, -1)], axis=-1) * jax.nn.silu(gc)
        xc = xc + gate_c * (yc @ w_out)
    return x, xc


def _odd_layer(x, xc, c, c_ctx, norm_g, w_ada, b_ada, w_in, q_norm_g, kv_norm_g, w_qb, w_kvb, rpb, w_out, cos, sin, ctx_out):
    B, S, _ = x.shape
    L = xc.shape[1]
    dd = D_HEAD_DIM
    h, gate = _modulate(x, c, norm_g, w_ada, b_ada)
    hc, gate_c = _modulate(xc, c_ctx[None, :], norm_g, w_ada, b_ada)
    cq, ckv, kr, qd, kd, vd, g = _split(h @ w_in, ODD_SPLITS)
    cqc, ckvc, krc, qdc, kdc, vdc, gc = _split(hc @ w_in, ODD_SPLITS)
    qm, km, vm = _mla_qkv(cq, ckv, kr, q_norm_g, kv_norm_g, w_qb, w_kvb, cos, sin)
    qmc, kmc, vmc = _mla_qkv(cqc, ckvc, krc, q_norm_g, kv_norm_g, w_qb, w_kvb, None, None)
    km_all = jnp.concatenate([km, kmc], axis=1)
    vm_all = jnp.concatenate([vm, vmc], axis=1)
    ym = _sweep_query_blocks(lambda n, qn: _softmax_attend(qn, km_all, vm_all), qm)
    kdc = kdc.reshape(B, L, D_HEADS, dd)
    vdc = vdc.reshape(B, L, D_HEADS, dd)
    yd = _neighbourhood_attend(qd.reshape(B, S, D_HEADS, dd), kd.reshape(B, S, D_HEADS, dd), vd.reshape(B, S, D_HEADS, dd), kdc, vdc, rpb)
    y = jnp.concatenate([ym.reshape(B, S, -1), yd.reshape(B, S, -1)], axis=-1) * jax.nn.silu(g)
    x = x + gate * (y @ w_out)
    if ctx_out:
        ymc = _softmax_attend(qmc, kmc, vmc)
        ydc = _softmax_attend(qdc.reshape(B, L, D_HEADS, dd), kdc, vdc)
        yc = jnp.concatenate([ymc.reshape(B, L, -1), ydc.reshape(B, L, -1)], axis=-1) * jax.nn.silu(gc)
        xc = xc + gate_c * (yc @ w_out)
    return x, xc


def setup_inputs(seed: int = 0) -> dict:
    key = jax.random.key(seed)
    keys = iter(jax.random.split(key, 24))

    def nrm(shape, std):
        return std * jax.random.normal(next(keys), shape, jnp.float32)

    D = D_MODEL
    n_ev = (DEPTH + 1) // 2
    n_od = DEPTH // 2
    return {
        'x': nrm((BATCH, SEQ, D), 1.0),
        'c': nrm((BATCH, D), 1.0),
        'ctx': nrm((BATCH, CTX_LEN, D), 1.0),
        'c_ctx': nrm((D,), 1.0),
        'ev_norm_g': 1.0 + nrm((n_ev, D), 0.02),
        'ev_w_ada': nrm((n_ev, D, 3 * D), D ** -0.5),
        'ev_b_ada': nrm((n_ev, 3 * D), 0.02),
        'ev_w_in': nrm((n_ev, D, IN_EVEN), D ** -0.5),
        'ev_a_sink': nrm((n_ev, A_HEADS), 0.5),
        'ev_b_lambda': nrm((n_ev, 4, B_HEAD_DIM), 0.1),
        'ev_b_subln_g': 1.0 + nrm((n_ev, 2 * B_HEAD_DIM), 0.02),
        'ev_w_out': nrm((n_ev, MIX_EVEN, D), MIX_EVEN ** -0.5),
        'od_norm_g': 1.0 + nrm((n_od, D), 0.02),
        'od_w_ada': nrm((n_od, D, 3 * D), D ** -0.5),
        'od_b_ada': nrm((n_od, 3 * D), 0.02),
        'od_w_in': nrm((n_od, D, IN_ODD), D ** -0.5),
        'od_c_q_norm_g': 1.0 + nrm((n_od, C_Q_RANK), 0.02),
        'od_c_kv_norm_g': 1.0 + nrm((n_od, C_KV_RANK), 0.02),
        'od_c_w_qb': nrm((n_od, C_Q_RANK, C_HEADS * (C_NOPE_DIM + C_ROPE_DIM)), C_Q_RANK ** -0.5),
        'od_c_w_kvb': nrm((n_od, C_KV_RANK, C_HEADS * (C_NOPE_DIM + C_V_DIM)), C_KV_RANK ** -0.5),
        'od_d_rpb': nrm((n_od, D_HEADS, 2 * D_WIN_ROWS - 1, 2 * D_WIN_COLS - 1), 0.02),
        'od_w_out': nrm((n_od, MIX_ODD, D), MIX_ODD ** -0.5),
        'final_norm_g': 1.0 + nrm((D,), 0.02),
    }


def reference(x, c, ctx, c_ctx, ev_norm_g, ev_w_ada, ev_b_ada, ev_w_in, ev_a_sink, ev_b_lambda, ev_b_subln_g, ev_w_out, od_norm_g, od_w_ada, od_b_ada, od_w_in, od_c_q_norm_g, od_c_kv_norm_g, od_c_w_qb, od_c_w_kvb, od_d_rpb, od_w_out, final_norm_g):
    S = x.shape[1]
    cos, sin = _axial_rope_tables(S, ROPE_DIM)
    xl, xc = x, ctx
    for i in range(DEPTH):
        ctx_out = i < DEPTH - 1
        j = i // 2
        if i % 2 == 0:
            xl, xc = _even_layer(xl, xc, c, c_ctx, i, ev_norm_g[j], ev_w_ada[j], ev_b_ada[j], ev_w_in[j], ev_a_sink[j], ev_b_lambda[j], ev_b_subln_g[j], ev_w_out[j], cos, sin, ctx_out)
        else:
            xl, xc = _odd_layer(xl, xc, c, c_ctx, od_norm_g[j], od_w_ada[j], od_b_ada[j], od_w_in[j], od_c_q_norm_g[j], od_c_kv_norm_g[j], od_c_w_qb[j], od_c_w_kvb[j], od_d_rpb[j], od_w_out[j], cos, sin, ctx_out)
    return rmsnorm(xl, final_norm_g)
```

```python
import functools
import math

import jax
import jax.numpy as jnp
from jax import lax
from jax.experimental import pallas as pl
from jax.experimental.pallas import tpu as pltpu

F32 = jnp.float32
BF16 = jnp.bfloat16

D = 1024
S = 16384
L = 256
T = S + L
GRID_W = 64
EPS = 1e-6
NEG = -1e30
ROPE_THETA = 10000.0
A_WINDOW = 128
LANES = 128
HEAD = 64
MLA_DK = 192
LAM_INIT = 0.8 - 0.6 * math.exp(-0.3 * 0)

TM = 256
N_LAT_TILES = S // TM
VMEM_LIMIT = 56 * 1024 * 1024

E_QA, E_KA, E_VA, E_QB, E_KB, E_VB, E_G = 0, 512, 768, 1024, 1536, 2048, 2560
E_COLS = 3584
O_QM, O_KM, O_VM, O_QD, O_KD, O_VD, O_G = 0, 1024, 2048, 2560, 3072, 3584, 4096
O_COLS = 5120


def _params(sem):
    return pltpu.CompilerParams(dimension_semantics=sem, vmem_limit_bytes=VMEM_LIMIT)


def _dot(a, b):
    return jnp.dot(a, b, preferred_element_type=F32)


def _dot_nt(a, b):
    return lax.dot_general(a, b, (((1,), (1,)), ((), ())), preferred_element_type=F32)


def _silu(z):
    return z / (1.0 + jnp.exp(-z))


def _rms(z):
    return z * lax.rsqrt(jnp.mean(z * z, axis=-1, keepdims=True) + EPS)


def _ada_kernel(ct_ref, w_ref, b_ref, o_ref):
    sc = _silu(ct_ref[...])
    w = w_ref[...]
    r0 = jnp.sum(w * sc[:, 0:1], axis=0, keepdims=True)
    r1 = jnp.sum(w * sc[:, 1:2], axis=0, keepdims=True)
    o_ref[...] = jnp.concatenate([r0, r1], axis=0) + b_ref[...]


def _ada(ct, w, b):
    tn = 768
    return pl.pallas_call(
        _ada_kernel,
        out_shape=jax.ShapeDtypeStruct((2, 3 * D), F32),
        grid=(3 * D // tn,),
        in_specs=[
            pl.BlockSpec((D, 2), lambda n: (0, 0)),
            pl.BlockSpec((D, tn), lambda n: (0, n)),
            pl.BlockSpec((1, tn), lambda n: (0, n)),
        ],
        out_specs=pl.BlockSpec((2, tn), lambda n: (0, n)),
        compiler_params=_params(("parallel",)),
        name="ada_mod",
    )(ct, w, b)


def _modulated(x_ref, ctx_ref, mod_ref, ng_ref):
    is_lat = pl.program_id(0) < N_LAT_TILES
    xin = jnp.where(is_lat, x_ref[...], ctx_ref[...])
    mod = mod_ref[...]
    modr = jnp.where(is_lat, mod[0:1], mod[1:2])
    shift, scale = modr[:, :D], modr[:, D:2 * D]
    return (_rms(xin) * ng_ref[...] * (1.0 + scale) + shift).astype(BF16)


def _rope_fn(cos_ref, sa_ref, sb_ref):
    cos, sa, sb = cos_ref[...], sa_ref[...], sb_ref[...]

    def rope(z):
        return z * cos + pltpu.roll(z, LANES - 16, 1) * sa + pltpu.roll(z, 16, 1) * sb

    return rope


def _proj_even_kernel(x_ref, ctx_ref, mod_ref, ng_ref, w_ref, cos_ref, sa_ref, sb_ref, o_ref):
    h = _modulated(x_ref, ctx_ref, mod_ref, ng_ref)
    res = _dot(h, w_ref[...])
    rope = _rope_fn(cos_ref, sa_ref, sb_ref)
    roped = ((E_QA, E_KA), (E_KA, E_VA), (E_QB, E_VB))
    for lo, hi in roped:
        for c0 in range(lo, hi, LANES):
            o_ref[:, c0:c0 + LANES] = rope(res[:, c0:c0 + LANES]).astype(BF16)
    o_ref[:, E_VA:E_QB] = res[:, E_VA:E_QB].astype(BF16)
    o_ref[:, E_VB:E_G] = res[:, E_VB:E_G].astype(BF16)
    o_ref[:, E_G:] = _silu(res[:, E_G:]).astype(BF16)


def _proj_odd_kernel(x_ref, ctx_ref, mod_ref, ng_ref, w_ref, qg_ref, kvg_ref, wqb_ref, wkvb_ref,
                     cos_ref, sa_ref, sb_ref, o_ref):
    h = _modulated(x_ref, ctx_ref, mod_ref, ng_ref)
    res = _dot(h, w_ref[...])
    rope = _rope_fn(cos_ref, sa_ref, sb_ref)
    qn = (_rms(res[:, 0:256]) * qg_ref[...]).astype(BF16)
    kvn = (_rms(res[:, 256:384]) * kvg_ref[...]).astype(BF16)
    q = _dot(qn, wqb_ref[...]) * (MLA_DK ** -0.5)
    kv = _dot(kvn, wkvb_ref[...])
    kpe = rope(res[:, 384:512]).astype(BF16)
    pes = (rope(q[:, 512:640]), rope(q[:, 640:768]))
    left = lax.broadcasted_iota(jnp.int32, (TM, LANES), 1) < HEAD
    for hd in range(4):
        pe = pes[hd // 2]
        pem = jnp.where(left if hd % 2 == 0 else jnp.logical_not(left), pe, 0.0)
        c0 = O_QM + 256 * hd
        o_ref[:, c0:c0 + 128] = q[:, 128 * hd:128 * hd + 128].astype(BF16)
        o_ref[:, c0 + 128:c0 + 256] = pem.astype(BF16)
        c0 = O_KM + 256 * hd
        o_ref[:, c0:c0 + 128] = kv[:, 256 * hd:256 * hd + 128].astype(BF16)
        o_ref[:, c0 + 128:c0 + 256] = kpe
        c0 = O_VM + 128 * hd
        o_ref[:, c0:c0 + 128] = kv[:, 256 * hd + 128:256 * hd + 256].astype(BF16)
    o_ref[:, O_QD:O_G] = res[:, 512:2048].astype(BF16)
    o_ref[:, O_G:] = _silu(res[:, 2048:3072]).astype(BF16)


def _row_specs(n_w_cols):
    const = lambda i: (0, 0)
    return [
        pl.BlockSpec((TM, D), lambda i: (jnp.minimum(i, N_LAT_TILES - 1), 0)),
        pl.BlockSpec((L, D), const),
        pl.BlockSpec((2, 3 * D), const),
        pl.BlockSpec((1, D), const),
        pl.BlockSpec((D, n_w_cols), const),
    ]


def _table_specs():
    return [pl.BlockSpec((TM, LANES), lambda i: (i, 0))] * 3


def _proj_even(x, ctx, mod, ng, w, tables):
    return pl.pallas_call(
        _proj_even_kernel,
        out_shape=jax.ShapeDtypeStruct((T, E_COLS), BF16),
        grid=(T // TM,),
        in_specs=_row_specs(E_COLS) + _table_specs(),
        out_specs=pl.BlockSpec((TM, E_COLS), lambda i: (i, 0)),
        compiler_params=_params(("parallel",)),
        name="proj_even",
    )(x, ctx, mod, ng, w, *tables)


def _proj_odd(x, ctx, mod, ng, w, qg, kvg, wqb, wkvb, tables):
    const = lambda i: (0, 0)
    return pl.pallas_call(
        _proj_odd_kernel,
        out_shape=jax.ShapeDtypeStruct((T, O_COLS), BF16),
        grid=(T // TM,),
        in_specs=_row_specs(3072) + [
            pl.BlockSpec((1, 256), const),
            pl.BlockSpec((1, 128), const),
            pl.BlockSpec((256, 768), const),
            pl.BlockSpec((128, 1024), const),
        ] + _table_specs(),
        out_specs=pl.BlockSpec((TM, O_COLS), lambda i: (i, 0)),
        compiler_params=_params(("parallel",)),
        name="proj_odd",
    )(x, ctx, mod, ng, w, qg, kvg, wqb, wkvb, *tables)


def _flash_kernel(*refs, n_half, nkv, tk):
    if n_half == 2:
        q_ref, k_ref, v_ref, g_ref, bl_ref, sg_ref, o_ref, m_ref, l_ref, acc_ref = refs
    else:
        q_ref, k_ref, v_ref, g_ref, o_ref, m_ref, l_ref, acc_ref = refs
    j = pl.program_id(2)

    @pl.when(j == 0)
    def _():
        m_ref[...] = jnp.full(m_ref.shape, NEG, F32)
        l_ref[...] = jnp.zeros(l_ref.shape, F32)
        acc_ref[...] = jnp.zeros(acc_ref.shape, F32)

    q = q_ref[...]
    k = k_ref[...]
    v = v_ref[...]
    if n_half == 2:
        qf = q.astype(F32)
        left = lax.broadcasted_iota(jnp.int32, qf.shape, 1) < HEAD
        qs = (jnp.where(left, qf, 0.0).astype(BF16), jnp.where(left, 0.0, qf).astype(BF16))
    else:
        qs = (q,)
    for t, qt in enumerate(qs):
        s = _dot_nt(qt, k)
        m_prev = m_ref[t]
        m_new = jnp.maximum(m_prev, jnp.max(s, axis=1, keepdims=True))
        alpha = jnp.exp(m_prev - m_new)
        p = jnp.exp(s - jnp.concatenate([m_new] * (tk // LANES), axis=1))
        l_ref[t] = alpha * l_ref[t] + jnp.sum(p, axis=1, keepdims=True)
        acc_ref[t] = alpha * acc_ref[t] + _dot(p.astype(BF16), v)
        m_ref[t] = m_new

    @pl.when(j == nkv - 1)
    def _():
        if n_half == 2:
            bl = bl_ref[...]
            lam = (jnp.exp(jnp.sum(bl[0:1] * bl[1:2], axis=1, keepdims=True))
                   - jnp.exp(jnp.sum(bl[2:3] * bl[3:4], axis=1, keepdims=True)) + LAM_INIT)
            o = acc_ref[0] / l_ref[0] - lam * (acc_ref[1] / l_ref[1])
            o = _rms(o) * sg_ref[...] * (1.0 - LAM_INIT)
        else:
            o = acc_ref[0] / l_ref[0]
        o_ref[...] = (o * g_ref[...].astype(F32)).astype(BF16)


def _flash(proj, extra, *, n_half, dq, qcol, kcol, vcol, gcol, n_heads, q_row0, n_q_rows, kv_row0, n_kv_rows,
           tq, tk, name):
    nq, nkv = n_q_rows // tq, n_kv_rows // tk
    qb0, kb0 = q_row0 // tq, kv_row0 // tk
    in_specs = [
        pl.BlockSpec((tq, dq), lambda h, i, j: (qb0 + i, qcol // dq + h)),
        pl.BlockSpec((tk, dq), lambda h, i, j: (kb0 + j, kcol // dq + h)),
        pl.BlockSpec((tk, LANES), lambda h, i, j: (kb0 + j, vcol // LANES + h)),
        pl.BlockSpec((tq, LANES), lambda h, i, j: (qb0 + i, gcol // LANES + h)),
    ]
    args = [proj, proj, proj, proj]
    if n_half == 2:
        in_specs += [pl.BlockSpec((4, HEAD), lambda h, i, j: (0, 0)),
                     pl.BlockSpec((1, LANES), lambda h, i, j: (0, 0))]
        args += list(extra)
    return pl.pallas_call(
        functools.partial(_flash_kernel, n_half=n_half, nkv=nkv, tk=tk),
        out_shape=jax.ShapeDtypeStruct((n_q_rows, n_heads * LANES), BF16),
        grid=(n_heads, nq, nkv),
        in_specs=in_specs,
        out_specs=pl.BlockSpec((tq, LANES), lambda h, i, j: (i, h)),
        scratch_shapes=[
            pltpu.VMEM((n_half, tq, LANES), F32),
            pltpu.VMEM((n_half, tq, LANES), F32),
            pltpu.VMEM((n_half, tq, LANES), F32),
        ],
        compiler_params=_params(("parallel", "parallel", "arbitrary")),
        name=name,
    )(*args)


def _local_blocks(i):
    last = T // TM - 1
    return jnp.maximum(i - 1, 0), i, jnp.minimum(i + 1, last), last


def _head_pair_merge(outs, g_ref, o_ref):
    left = lax.broadcasted_iota(jnp.int32, (TM, LANES), 1) < HEAD
    for sl in range(len(outs) // 2):
        o = jnp.where(left, outs[2 * sl], outs[2 * sl + 1])
        c0 = sl * LANES
        o_ref[:, c0:c0 + LANES] = (o * g_ref[:, c0:c0 + LANES].astype(F32)).astype(BF16)


def _masked_heads(q_ref, n_heads):
    left = lax.broadcasted_iota(jnp.int32, (TM, LANES), 1) < HEAD
    out = []
    for h in range(n_heads):
        qf = q_ref[:, (h // 2) * LANES:(h // 2 + 1) * LANES].astype(F32)
        out.append(jnp.where(left if h % 2 == 0 else jnp.logical_not(left), qf, 0.0).astype(BF16))
    return out


def _mixer_a_kernel(sink_ref, q_ref, k0_ref, k1_ref, k2_ref, kc_ref, v0_ref, v1_ref, v2_ref, vc_ref, g_ref, o_ref):
    i = pl.program_id(0)
    n_loc = 3 * TM
    k_all = jnp.concatenate([k0_ref[...], k1_ref[...], k2_ref[...], kc_ref[...]], axis=0)
    v_all = jnp.concatenate([v0_ref[...], v1_ref[...], v2_ref[...], vc_ref[...]], axis=0)
    qpos = i * TM + lax.broadcasted_iota(jnp.int32, (TM, n_loc + L), 0)
    col = lax.broadcasted_iota(jnp.int32, (TM, n_loc + L), 1)
    kpos = (i - 1) * TM + col
    in_band = jnp.where(jnp.abs(qpos - kpos) <= A_WINDOW, 1, 0) * jnp.where(kpos >= 0, 1, 0) \
        * jnp.where(kpos < S, 1, 0) * jnp.where(qpos < S, 1, 0)
    valid = jnp.where(col >= n_loc, 1, in_band) > 0
    qs = _masked_heads(q_ref, 8)
    outs = []
    for hq in range(8):
        kh = hq // 4
        s = _dot_nt(qs[hq], k_all[:, kh * LANES:(kh + 1) * LANES])
        s = jnp.where(valid, s, NEG)
        snk = sink_ref[hq]
        m = jnp.maximum(jnp.max(s, axis=1, keepdims=True), snk)
        p = jnp.exp(s - m)
        denom = jnp.sum(p, axis=1, keepdims=True) + jnp.exp(snk - m)
        o = _dot(p.astype(BF16), v_all[:, kh * LANES:(kh + 1) * LANES])
        outs.append(o / denom)
    _head_pair_merge(outs, g_ref, o_ref)


def _mixer_a(proj, sink):
    nt = T // TM

    def kv_spec(which, col_block):
        return pl.BlockSpec((TM, 256), lambda i: (_local_blocks(i)[which], col_block))

    return pl.pallas_call(
        _mixer_a_kernel,
        out_shape=jax.ShapeDtypeStruct((T, 512), BF16),
        grid=(nt,),
        in_specs=[pl.BlockSpec(memory_space=pltpu.SMEM),
                  pl.BlockSpec((TM, 512), lambda i: (i, E_QA // 512))]
        + [kv_spec(w, E_KA // 256) for w in range(4)]
        + [kv_spec(w, E_VA // 256) for w in range(4)]
        + [pl.BlockSpec((TM, 512), lambda i: (i, E_G // 512))],
        out_specs=pl.BlockSpec((TM, 512), lambda i: (i, 0)),
        compiler_params=_params(("parallel",)),
        name="mixer_a",
    )(sink, proj, *([proj] * 8), proj)


def _mixer_d_kernel(q_ref, k0_ref, k1_ref, k2_ref, kc_ref, v0_ref, v1_ref, v2_ref, vc_ref, bias_ref, g_ref, o_ref):
    n_loc = 3 * TM
    k_all = jnp.concatenate([k0_ref[...], k1_ref[...], k2_ref[...], kc_ref[...]], axis=0)
    v_all = jnp.concatenate([v0_ref[...], v1_ref[...], v2_ref[...], vc_ref[...]], axis=0)
    qs = _masked_heads(q_ref, 8)
    outs = []
    for h in range(8):
        sl = h // 2
        s = _dot_nt(qs[h], k_all[:, sl * LANES:(sl + 1) * LANES])
        s = jnp.concatenate([s[:, :n_loc] + bias_ref[0, h], s[:, n_loc:]], axis=1)
        m = jnp.max(s, axis=1, keepdims=True)
        p = jnp.exp(s - m)
        denom = jnp.sum(p, axis=1, keepdims=True)
        o = _dot(p.astype(BF16), v_all[:, sl * LANES:(sl + 1) * LANES])
        outs.append(o / denom)
    _head_pair_merge(outs, g_ref, o_ref)


def _mixer_d(proj, bias):
    nt = S // TM

    def kv_spec(which, col_block):
        return pl.BlockSpec((TM, 512), lambda i: (_local_blocks(i)[which], col_block))

    def variant(i):
        return jnp.where(i == 0, 0, jnp.where(i == nt - 1, 2, 1))

    return pl.pallas_call(
        _mixer_d_kernel,
        out_shape=jax.ShapeDtypeStruct((S, 512), BF16),
        grid=(nt,),
        in_specs=[pl.BlockSpec((TM, 512), lambda i: (i, O_QD // 512))]
        + [kv_spec(w, O_KD // 512) for w in range(4)]
        + [kv_spec(w, O_VD // 512) for w in range(4)]
        + [pl.BlockSpec((1, 8, TM, 3 * TM), lambda i: (variant(i), 0, 0, 0)),
           pl.BlockSpec((TM, 512), lambda i: (i, O_G // 512 + 1))],
        out_specs=pl.BlockSpec((TM, 512), lambda i: (i, 0)),
        compiler_params=_params(("parallel",)),
        name="mixer_d",
    )(proj, *([proj] * 8), bias, proj)


def _neighbourhood_bias(rpb):
    ri = jnp.arange(TM) // GRID_W
    qc = jnp.arange(TM) % GRID_W
    krel = jnp.arange(3 * TM) // GRID_W
    kc = jnp.arange(3 * TM) % GRID_W
    dr = krel[None, :] - ri[:, None] + 3
    dc = kc[None, :] - qc[:, None] + 15
    cs = jnp.clip(qc - 8, 0, GRID_W - 16)
    col_ok = (kc[None, :] >= cs[:, None]) & (kc[None, :] < cs[:, None] + 16)
    b = rpb[:, jnp.clip(dr, 0, 14), jnp.clip(dc, 0, 30)]
    off = krel[None, :] - ri[:, None]
    row_first = jnp.broadcast_to((krel >= 4)[None, :], off.shape)
    row_mid = (off >= 0) & (off < 8)
    row_last = jnp.broadcast_to((krel < 8)[None, :], off.shape)
    return jnp.stack([jnp.where((r & col_ok)[None], b, NEG) for r in (row_first, row_mid, row_last)])


def _out_kernel(*refs, mod_row, final):
    if final:
        x_ref, y1_ref, y2_ref, w_ref, gate_ref, fg_ref, o_ref = refs
    else:
        x_ref, y1_ref, y2_ref, w_ref, gate_ref, o_ref = refs
    y = jnp.concatenate([y1_ref[...], y2_ref[...]], axis=1)
    xn = x_ref[...] + gate_ref[mod_row:mod_row + 1, :] * _dot(y, w_ref[...])
    if final:
        xn = _rms(xn) * fg_ref[...]
    o_ref[...] = xn


def _out_proj(x, y1, y1_row0, y2, w, mod, mod_row, final_g=None):
    n_rows = x.shape[0]
    b0 = y1_row0 // TM
    const = lambda i: (0, 0)
    in_specs = [
        pl.BlockSpec((TM, D), lambda i: (i, 0)),
        pl.BlockSpec((TM, 512), lambda i: (b0 + i, 0)),
        pl.BlockSpec((TM, 512), lambda i: (i, 0)),
        pl.BlockSpec((D, D), const),
        pl.BlockSpec((2, D), lambda i: (0, 2)),
    ]
    args = [x, y1, y2, w, mod]
    if final_g is not None:
        in_specs.append(pl.BlockSpec((1, D), const))
        args.append(final_g)
    return pl.pallas_call(
        functools.partial(_out_kernel, mod_row=mod_row, final=final_g is not None),
        out_shape=jax.ShapeDtypeStruct((n_rows, D), F32),
        grid=(n_rows // TM,),
        in_specs=in_specs,
        out_specs=pl.BlockSpec((TM, D), lambda i: (i, 0)),
        compiler_params=_params(("parallel",)),
        name="out_proj",
    )(*args)


def _rope_tables():
    t = jnp.arange(S)
    row = (t // GRID_W).astype(F32)
    col = (t % GRID_W).astype(F32)
    quarter = HEAD // 4
    inv = ROPE_THETA ** (-jnp.arange(quarter, dtype=F32) / quarter)
    ang_r = row[:, None] * inv[None, :]
    ang_c = col[:, None] * inv[None, :]
    ang = jnp.concatenate([ang_r, ang_r, ang_c, ang_c], axis=-1)
    cos, sin = jnp.cos(ang), jnp.sin(ang)
    first = (jnp.arange(HEAD) % 32) < 16
    sa = jnp.where(first, -sin, 0.0)
    sb = jnp.where(first, 0.0, sin)
    pad = lambda a, v: jnp.tile(jnp.concatenate([a, jnp.full((L, HEAD), v, F32)], axis=0), (1, 2))
    return pad(cos, 1.0), pad(sa, 0.0), pad(sb, 0.0)


def kernel(x, c, ctx, c_ctx, ev_norm_g, ev_w_ada, ev_b_ada, ev_w_in, ev_a_sink, ev_b_lambda, ev_b_subln_g, ev_w_out, od_norm_g, od_w_ada, od_b_ada, od_w_in, od_c_q_norm_g, od_c_kv_norm_g, od_c_w_qb, od_c_w_kvb, od_d_rpb, od_w_out, final_norm_g):
    assert x.shape == (1, S, D) and ctx.shape == (1, L, D)
    assert ev_w_in.shape[0] == 1 and od_w_in.shape[0] == 1
    x0, xc0 = x[0], ctx[0]
    ct = jnp.stack([c[0], c_ctx], axis=1)
    tables = _rope_tables()
    qk_scale = HEAD ** -0.5

    mod = _ada(ct, ev_w_ada[0], ev_b_ada[0][None])
    w = ev_w_in[0]
    k0, k1, v0, v1 = w[:, 512:576], w[:, 576:640], w[:, 640:704], w[:, 704:768]
    w_e = jnp.concatenate([w[:, 0:512] * qk_scale, k0, k0, k1, k1, v0, v0, v1, v1,
                           w[:, 768:1280] * qk_scale, w[:, 1280:]], axis=1).astype(BF16)
    proj = _proj_even(x0, xc0, mod, ev_norm_g[0][None], w_e, tables)
    ya = _mixer_a(proj, ev_a_sink[0])
    diff_extra = (ev_b_lambda[0], ev_b_subln_g[0][None])
    diff = functools.partial(_flash, proj, diff_extra, n_half=2, dq=128, qcol=E_QB, kcol=E_KB, vcol=E_VB,
                             gcol=E_G + 512, n_heads=4)
    yb = diff(q_row0=0, n_q_rows=S, kv_row0=0, n_kv_rows=T, tq=512, tk=1280, name="diff_attn")
    ybc = diff(q_row0=S, n_q_rows=L, kv_row0=S, n_kv_rows=L, tq=L, tk=L, name="diff_attn_ctx")
    w_out = ev_w_out[0].astype(BF16)
    x1 = _out_proj(x0, ya, 0, yb, w_out, mod, 0)
    xc1 = _out_proj(xc0, ya, S, ybc, w_out, mod, 1)

    mod = _ada(ct, od_w_ada[0], od_b_ada[0][None])
    w = od_w_in[0]
    w_o = jnp.concatenate([w[:, 0:384], w[:, 384:448], w[:, 384:448], w[:, 448:960] * qk_scale, w[:, 960:]],
                          axis=1).astype(BF16)
    wqb = od_c_w_qb[0].reshape(256, 4, MLA_DK)
    wqb = jnp.concatenate([wqb[:, :, :128].reshape(256, 512), wqb[:, :, 128:].reshape(256, 256)],
                          axis=1).astype(BF16)
    proj = _proj_odd(x1, xc1, mod, od_norm_g[0][None], w_o, od_c_q_norm_g[0][None], od_c_kv_norm_g[0][None],
                     wqb, od_c_w_kvb[0].astype(BF16), tables)
    ym = _flash(proj, (), n_half=1, dq=256, qcol=O_QM, kcol=O_KM, vcol=O_VM, gcol=O_G, n_heads=4,
                q_row0=0, n_q_rows=S, kv_row0=0, n_kv_rows=T, tq=512, tk=1280, name="mla_attn")
    yd = _mixer_d(proj, _neighbourhood_bias(od_d_rpb[0]))
    out = _out_proj(x1, ym, 0, yd, od_w_out[0].astype(BF16), mod, 0, final_norm_g[None])
    return out[None]
```

```python
import functools
import math

import jax
import jax.numpy as jnp
from jax import lax
from jax.experimental import pallas as pl
from jax.experimental.pallas import tpu as pltpu

F32 = jnp.float32
BF16 = jnp.bfloat16

D = 1024
S = 16384
L = 256
T = S + L
GRID_W = 64
EPS = 1e-6
NEG = -1e30
ROPE_THETA = 10000.0
A_WINDOW = 128
LANES = 128
HEAD = 64
MLA_DK = 192
LAM_INIT = 0.8 - 0.6 * math.exp(-0.3 * 0)

LOG2E = math.log2(math.e)
FLASH_TQ = 512
FLASH_TK = 1024
FLASH_UNROLL = 8

TM = 256
N_LAT_TILES = S // TM
VMEM_LIMIT = 56 * 1024 * 1024

E_QA, E_KA, E_VA, E_QB, E_KB, E_VB, E_G = 0, 512, 768, 1024, 1536, 2048, 2560
E_COLS = 3584
O_QM, O_KM, O_VM, O_QD, O_KD, O_VD, O_G = 0, 1024, 2048, 2560, 3072, 3584, 4096
O_COLS = 5120


def _params(sem):
    return pltpu.CompilerParams(dimension_semantics=sem, vmem_limit_bytes=VMEM_LIMIT)


def _dot(a, b):
    return jnp.dot(a, b, preferred_element_type=F32)


def _dot_nt(a, b):
    return lax.dot_general(a, b, (((1,), (1,)), ((), ())), preferred_element_type=F32)


def _silu(z):
    return z / (1.0 + jnp.exp(-z))


def _rms(z):
    return z * lax.rsqrt(jnp.mean(z * z, axis=-1, keepdims=True) + EPS)


def _ada_kernel(ct_ref, w_ref, b_ref, o_ref):
    sc = _silu(ct_ref[...])
    w = w_ref[...]
    r0 = jnp.sum(w * sc[:, 0:1], axis=0, keepdims=True)
    r1 = jnp.sum(w * sc[:, 1:2], axis=0, keepdims=True)
    o_ref[...] = jnp.concatenate([r0, r1], axis=0) + b_ref[...]


def _ada(ct, w, b):
    tn = 768
    return pl.pallas_call(
        _ada_kernel,
        out_shape=jax.ShapeDtypeStruct((2, 3 * D), F32),
        grid=(3 * D // tn,),
        in_specs=[
            pl.BlockSpec((D, 2), lambda n: (0, 0)),
            pl.BlockSpec((D, tn), lambda n: (0, n)),
            pl.BlockSpec((1, tn), lambda n: (0, n)),
        ],
        out_specs=pl.BlockSpec((2, tn), lambda n: (0, n)),
        compiler_params=_params(("parallel",)),
        name="ada_mod",
    )(ct, w, b)


def _modulated(x_ref, ctx_ref, mod_ref, ng_ref):
    is_lat = pl.program_id(0) < N_LAT_TILES
    xin = jnp.where(is_lat, x_ref[...], ctx_ref[...])
    mod = mod_ref[...]
    modr = jnp.where(is_lat, mod[0:1], mod[1:2])
    shift, scale = modr[:, :D], modr[:, D:2 * D]
    return (_rms(xin) * ng_ref[...] * (1.0 + scale) + shift).astype(BF16)


def _rope_fn(cos_ref, sa_ref, sb_ref):
    cos, sa, sb = cos_ref[...], sa_ref[...], sb_ref[...]

    def rope(z):
        return z * cos + pltpu.roll(z, LANES - 16, 1) * sa + pltpu.roll(z, 16, 1) * sb

    return rope


def _proj_even_kernel(x_ref, ctx_ref, mod_ref, ng_ref, w_ref, cos_ref, sa_ref, sb_ref, o_ref):
    h = _modulated(x_ref, ctx_ref, mod_ref, ng_ref)
    res = _dot(h, w_ref[...])
    rope = _rope_fn(cos_ref, sa_ref, sb_ref)
    roped = ((E_QA, E_VA, 1.0), (E_QB, E_KB, LOG2E), (E_KB, E_VB, 1.0))
    for lo, hi, mult in roped:
        for c0 in range(lo, hi, LANES):
            z = rope(res[:, c0:c0 + LANES])
            o_ref[:, c0:c0 + LANES] = (z if mult == 1.0 else z * mult).astype(BF16)
    o_ref[:, E_VA:E_QB] = res[:, E_VA:E_QB].astype(BF16)
    o_ref[:, E_VB:E_G] = res[:, E_VB:E_G].astype(BF16)
    o_ref[:, E_G:] = _silu(res[:, E_G:]).astype(BF16)


def _proj_odd_kernel(x_ref, ctx_ref, mod_ref, ng_ref, w_ref, qg_ref, kvg_ref, wqb_ref, wkvb_ref,
                     cos_ref, sa_ref, sb_ref, o_ref):
    h = _modulated(x_ref, ctx_ref, mod_ref, ng_ref)
    res = _dot(h, w_ref[...])
    rope = _rope_fn(cos_ref, sa_ref, sb_ref)
    qn = (_rms(res[:, 0:256]) * qg_ref[...]).astype(BF16)
    kvn = (_rms(res[:, 256:384]) * kvg_ref[...]).astype(BF16)
    q = _dot(qn, wqb_ref[...]) * (MLA_DK ** -0.5 * LOG2E)
    kv = _dot(kvn, wkvb_ref[...])
    kpe = rope(res[:, 384:512]).astype(BF16)
    pes = (rope(q[:, 512:640]), rope(q[:, 640:768]))
    left = lax.broadcasted_iota(jnp.int32, (TM, LANES), 1) < HEAD
    for hd in range(4):
        pe = pes[hd // 2]
        pem = jnp.where(left if hd % 2 == 0 else jnp.logical_not(left), pe, 0.0)
        c0 = O_QM + 256 * hd
        o_ref[:, c0:c0 + 128] = q[:, 128 * hd:128 * hd + 128].astype(BF16)
        o_ref[:, c0 + 128:c0 + 256] = pem.astype(BF16)
        c0 = O_KM + 256 * hd
        o_ref[:, c0:c0 + 128] = kv[:, 256 * hd:256 * hd + 128].astype(BF16)
        o_ref[:, c0 + 128:c0 + 256] = kpe
        c0 = O_VM + 128 * hd
        o_ref[:, c0:c0 + 128] = kv[:, 256 * hd + 128:256 * hd + 256].astype(BF16)
    o_ref[:, O_QD:O_G] = res[:, 512:2048].astype(BF16)
    o_ref[:, O_G:] = _silu(res[:, 2048:3072]).astype(BF16)


def _row_specs(n_w_cols):
    const = lambda i: (0, 0)
    return [
        pl.BlockSpec((TM, D), lambda i: (jnp.minimum(i, N_LAT_TILES - 1), 0)),
        pl.BlockSpec((L, D), const),
        pl.BlockSpec((2, 3 * D), const),
        pl.BlockSpec((1, D), const),
        pl.BlockSpec((D, n_w_cols), const),
    ]


def _table_specs():
    return [pl.BlockSpec((TM, LANES), lambda i: (i, 0))] * 3


def _proj_even(x, ctx, mod, ng, w, tables):
    return pl.pallas_call(
        _proj_even_kernel,
        out_shape=jax.ShapeDtypeStruct((T, E_COLS), BF16),
        grid=(T // TM,),
        in_specs=_row_specs(E_COLS) + _table_specs(),
        out_specs=pl.BlockSpec((TM, E_COLS), lambda i: (i, 0)),
        compiler_params=_params(("parallel",)),
        name="proj_even",
    )(x, ctx, mod, ng, w, *tables)


def _proj_odd(x, ctx, mod, ng, w, qg, kvg, wqb, wkvb, tables):
    const = lambda i: (0, 0)
    return pl.pallas_call(
        _proj_odd_kernel,
        out_shape=jax.ShapeDtypeStruct((T, O_COLS), BF16),
        grid=(T // TM,),
        in_specs=_row_specs(3072) + [
            pl.BlockSpec((1, 256), const),
            pl.BlockSpec((1, 128), const),
            pl.BlockSpec((256, 768), const),
            pl.BlockSpec((128, 1024), const),
        ] + _table_specs(),
        out_specs=pl.BlockSpec((TM, O_COLS), lambda i: (i, 0)),
        compiler_params=_params(("parallel",)),
        name="proj_odd",
    )(x, ctx, mod, ng, w, qg, kvg, wqb, wkvb, *tables)


def _flash_kernel(*refs, n_half, n_chunks, tk, tail, unroll):
    if n_half == 2:
        q_ref, k_ref, v_ref, g_ref, bl_ref, sg_ref, o_ref = refs
    else:
        q_ref, k_ref, v_ref, g_ref, o_ref = refs
    q = q_ref[...]
    tq = q.shape[0]
    if n_half == 2:
        qf = q.astype(F32)
        left = lax.broadcasted_iota(jnp.int32, qf.shape, 1) < HEAD
        qs = (jnp.where(left, qf, 0.0).astype(BF16), jnp.where(left, 0.0, qf).astype(BF16))
    else:
        qs = (q,)

    def update(r0, n, carry):
        k = k_ref[pl.ds(r0, n), :]
        v = v_ref[pl.ds(r0, n), :]
        new = []
        for t in range(n_half):
            m_prev, l_prev, acc = carry[t]
            s = _dot_nt(qs[t], k)
            m_new = jnp.maximum(m_prev, jnp.max(s, axis=1, keepdims=True))
            alpha = jnp.exp2(m_prev - m_new)
            p = jnp.exp2(s - jnp.concatenate([m_new] * (n // LANES), axis=1))
            l_new = alpha * l_prev + jnp.sum(p, axis=1, keepdims=True)
            acc = alpha * acc + _dot(p.astype(BF16), v)
            new.append((m_new, l_new, acc))
        return tuple(new)

    carry = tuple((jnp.full((tq, LANES), NEG, F32), jnp.zeros((tq, LANES), F32), jnp.zeros((tq, LANES), F32))
                  for _ in range(n_half))
    if n_chunks == 1:
        carry = update(0, tk, carry)
    else:
        carry = lax.fori_loop(0, n_chunks, lambda c, cr: update(pl.multiple_of(c * tk, tk), tk, cr), carry,
                              unroll=unroll)
    if tail:
        carry = update(n_chunks * tk, tail, carry)
    fin = carry
    if n_half == 2:
        bl = bl_ref[...]
        lam = (jnp.exp(jnp.sum(bl[0:1] * bl[1:2], axis=1, keepdims=True))
               - jnp.exp(jnp.sum(bl[2:3] * bl[3:4], axis=1, keepdims=True)) + LAM_INIT)
        o = fin[0][2] / fin[0][1] - lam * (fin[1][2] / fin[1][1])
        o = _rms(o) * sg_ref[...] * (1.0 - LAM_INIT)
    else:
        o = fin[0][2] / fin[0][1]
    o_ref[...] = (o * g_ref[...].astype(F32)).astype(BF16)


def _flash(proj, extra, *, n_half, dq, qcol, kcol, vcol, gcol, n_heads, q_row0, n_q_rows, kv_row0, n_kv_rows,
           tq, tk, unroll, name):
    nq, n_chunks = n_q_rows // tq, n_kv_rows // tk
    tail = n_kv_rows - n_chunks * tk
    qb0, kb0 = q_row0 // tq, kv_row0 // n_kv_rows
    in_specs = [
        pl.BlockSpec((tq, dq), lambda h, i: (qb0 + i, qcol // dq + h)),
        pl.BlockSpec((n_kv_rows, dq), lambda h, i: (kb0, kcol // dq + h), pipeline_mode=pl.Buffered(1)),
        pl.BlockSpec((n_kv_rows, LANES), lambda h, i: (kb0, vcol // LANES + h), pipeline_mode=pl.Buffered(1)),
        pl.BlockSpec((tq, LANES), lambda h, i: (qb0 + i, gcol // LANES + h)),
    ]
    args = [proj, proj, proj, proj]
    if n_half == 2:
        in_specs += [pl.BlockSpec((4, HEAD), lambda h, i: (0, 0)),
                     pl.BlockSpec((1, LANES), lambda h, i: (0, 0))]
        args += list(extra)
    return pl.pallas_call(
        functools.partial(_flash_kernel, n_half=n_half, n_chunks=n_chunks, tk=tk, tail=tail, unroll=unroll),
        out_shape=jax.ShapeDtypeStruct((n_q_rows, n_heads * LANES), BF16),
        grid=(n_heads, nq),
        in_specs=in_specs,
        out_specs=pl.BlockSpec((tq, LANES), lambda h, i: (i, h)),
        compiler_params=_params(("parallel", "parallel")),
        name=name,
    )(*args)


def _local_blocks(i):
    last = T // TM - 1
    return jnp.maximum(i - 1, 0), i, jnp.minimum(i + 1, last), last


def _head_pair_merge(outs, g_ref, o_ref):
    left = lax.broadcasted_iota(jnp.int32, (TM, LANES), 1) < HEAD
    for sl in range(len(outs) // 2):
        o = jnp.where(left, outs[2 * sl], outs[2 * sl + 1])
        c0 = sl * LANES
        o_ref[:, c0:c0 + LANES] = (o * g_ref[:, c0:c0 + LANES].astype(F32)).astype(BF16)


def _masked_heads(q_ref, n_heads):
    left = lax.broadcasted_iota(jnp.int32, (TM, LANES), 1) < HEAD
    out = []
    for h in range(n_heads):
        qf = q_ref[:, (h // 2) * LANES:(h // 2 + 1) * LANES].astype(F32)
        out.append(jnp.where(left if h % 2 == 0 else jnp.logical_not(left), qf, 0.0).astype(BF16))
    return out


def _mixer_a_kernel(sink_ref, q_ref, k0_ref, k1_ref, k2_ref, kc_ref, v0_ref, v1_ref, v2_ref, vc_ref, g_ref, o_ref):
    i = pl.program_id(0)
    n_loc = 3 * TM
    k_all = jnp.concatenate([k0_ref[...], k1_ref[...], k2_ref[...], kc_ref[...]], axis=0)
    v_all = jnp.concatenate([v0_ref[...], v1_ref[...], v2_ref[...], vc_ref[...]], axis=0)
    qpos = i * TM + lax.broadcasted_iota(jnp.int32, (TM, n_loc + L), 0)
    col = lax.broadcasted_iota(jnp.int32, (TM, n_loc + L), 1)
    kpos = (i - 1) * TM + col
    in_band = jnp.where(jnp.abs(qpos - kpos) <= A_WINDOW, 1, 0) * jnp.where(kpos >= 0, 1, 0) \
        * jnp.where(kpos < S, 1, 0) * jnp.where(qpos < S, 1, 0)
    valid = jnp.where(col >= n_loc, 1, in_band) > 0
    qs = _masked_heads(q_ref, 8)
    outs = []
    for hq in range(8):
        kh = hq // 4
        s = _dot_nt(qs[hq], k_all[:, kh * LANES:(kh + 1) * LANES])
        s = jnp.where(valid, s, NEG)
        snk = sink_ref[hq]
        m = jnp.maximum(jnp.max(s, axis=1, keepdims=True), snk)
        p = jnp.exp(s - m)
        denom = jnp.sum(p, axis=1, keepdims=True) + jnp.exp(snk - m)
        o = _dot(p.astype(BF16), v_all[:, kh * LANES:(kh + 1) * LANES])
        outs.append(o / denom)
    _head_pair_merge(outs, g_ref, o_ref)


def _mixer_a(proj, sink):
    nt = T // TM

    def kv_spec(which, col_block):
        return pl.BlockSpec((TM, 256), lambda i: (_local_blocks(i)[which], col_block))

    return pl.pallas_call(
        _mixer_a_kernel,
        out_shape=jax.ShapeDtypeStruct((T, 512), BF16),
        grid=(nt,),
        in_specs=[pl.BlockSpec(memory_space=pltpu.SMEM),
                  pl.BlockSpec((TM, 512), lambda i: (i, E_QA // 512))]
        + [kv_spec(w, E_KA // 256) for w in range(4)]
        + [kv_spec(w, E_VA // 256) for w in range(4)]
        + [pl.BlockSpec((TM, 512), lambda i: (i, E_G // 512))],
        out_specs=pl.BlockSpec((TM, 512), lambda i: (i, 0)),
        compiler_params=_params(("parallel",)),
        name="mixer_a",
    )(sink, proj, *([proj] * 8), proj)


def _mixer_d_kernel(q_ref, k0_ref, k1_ref, k2_ref, kc_ref, v0_ref, v1_ref, v2_ref, vc_ref, bias_ref, g_ref, o_ref):
    n_loc = 3 * TM
    k_all = jnp.concatenate([k0_ref[...], k1_ref[...], k2_ref[...], kc_ref[...]], axis=0)
    v_all = jnp.concatenate([v0_ref[...], v1_ref[...], v2_ref[...], vc_ref[...]], axis=0)
    qs = _masked_heads(q_ref, 8)
    outs = []
    for h in range(8):
        sl = h // 2
        s = _dot_nt(qs[h], k_all[:, sl * LANES:(sl + 1) * LANES])
        s = jnp.concatenate([s[:, :n_loc] + bias_ref[0, h], s[:, n_loc:]], axis=1)
        m = jnp.max(s, axis=1, keepdims=True)
        p = jnp.exp(s - m)
        denom = jnp.sum(p, axis=1, keepdims=True)
        o = _dot(p.astype(BF16), v_all[:, sl * LANES:(sl + 1) * LANES])
        outs.append(o / denom)
    _head_pair_merge(outs, g_ref, o_ref)


def _mixer_d(proj, bias):
    nt = S // TM

    def kv_spec(which, col_block):
        return pl.BlockSpec((TM, 512), lambda i: (_local_blocks(i)[which], col_block))

    def variant(i):
        return jnp.where(i == 0, 0, jnp.where(i == nt - 1, 2, 1))

    return pl.pallas_call(
        _mixer_d_kernel,
        out_shape=jax.ShapeDtypeStruct((S, 512), BF16),
        grid=(nt,),
        in_specs=[pl.BlockSpec((TM, 512), lambda i: (i, O_QD // 512))]
        + [kv_spec(w, O_KD // 512) for w in range(4)]
        + [kv_spec(w, O_VD // 512) for w in range(4)]
        + [pl.BlockSpec((1, 8, TM, 3 * TM), lambda i: (variant(i), 0, 0, 0)),
           pl.BlockSpec((TM, 512), lambda i: (i, O_G // 512 + 1))],
        out_specs=pl.BlockSpec((TM, 512), lambda i: (i, 0)),
        compiler_params=_params(("parallel",)),
        name="mixer_d",
    )(proj, *([proj] * 8), bias, proj)


D_ROWS, D_COLS = 8, 16
RPB_R, RPB_C = 2 * D_ROWS - 1, 2 * D_COLS - 1
Q_GRID_ROWS = TM // GRID_W
K_GRID_ROWS = 3 * Q_GRID_ROWS


def _bias_kernel(rpb_ref, o_ref):
    base = pl.program_id(0) * (RPB_R * RPB_C)
    qc = lax.broadcasted_iota(jnp.int32, (GRID_W, LANES), 0)
    lane = lax.broadcasted_iota(jnp.int32, (GRID_W, LANES), 1)
    right = lane >= GRID_W
    kc = jnp.where(right, lane - GRID_W, lane)
    cs = jnp.clip(qc - D_COLS // 2, 0, GRID_W - D_COLS)
    col_ok = jnp.where(kc >= cs, 1, 0) * jnp.where(kc < cs + D_COLS, 1, 0)
    dc = jnp.where(col_ok > 0, kc - qc + (D_COLS - 1), -1)
    dc_l = jnp.where(right, -1, dc)
    dc_r = jnp.where(right, dc, -1)
    neg = jnp.full((GRID_W, LANES), NEG, F32)
    pairs = []
    for d in range(RPB_R - 1):
        t2 = neg
        for j in range(RPB_C):
            t2 = jnp.where(dc_l == j, rpb_ref[base + d * RPB_C + j], t2)
            t2 = jnp.where(dc_r == j, rpb_ref[base + (d + 1) * RPB_C + j], t2)
        pairs.append(t2)
    row_ok = (lambda ri, kr: kr >= Q_GRID_ROWS,
              lambda ri, kr: 0 <= kr - ri < D_ROWS,
              lambda ri, kr: kr < D_ROWS)
    for var in range(3):
        for ri in range(Q_GRID_ROWS):
            for a in range(K_GRID_ROWS // 2):
                blk = pairs[2 * a - ri + 3]
                ok_l, ok_r = row_ok[var](ri, 2 * a), row_ok[var](ri, 2 * a + 1)
                if not ok_l and not ok_r:
                    blk = neg
                elif not ok_l:
                    blk = jnp.where(right, blk, NEG)
                elif not ok_r:
                    blk = jnp.where(right, NEG, blk)
                o_ref[var, 0, ri * GRID_W:(ri + 1) * GRID_W, a * LANES:(a + 1) * LANES] = blk


def _neighbourhood_bias(rpb):
    n_heads = rpb.shape[0]
    return pl.pallas_call(
        _bias_kernel,
        out_shape=jax.ShapeDtypeStruct((3, n_heads, TM, 3 * TM), F32),
        grid=(n_heads,),
        in_specs=[pl.BlockSpec(memory_space=pltpu.SMEM)],
        out_specs=pl.BlockSpec((3, 1, TM, 3 * TM), lambda h: (0, h, 0, 0)),
        compiler_params=_params(("parallel",)),
        name="nbr_bias",
    )(rpb.reshape(-1))


def _out_kernel(*refs, mod_row, final):
    if final:
        x_ref, y1_ref, y2_ref, w_ref, gate_ref, fg_ref, o_ref = refs
    else:
        x_ref, y1_ref, y2_ref, w_ref, gate_ref, o_ref = refs
    y = jnp.concatenate([y1_ref[...], y2_ref[...]], axis=1)
    xn = x_ref[...] + gate_ref[mod_row:mod_row + 1, :] * _dot(y, w_ref[...])
    if final:
        xn = _rms(xn) * fg_ref[...]
    o_ref[...] = xn


def _out_proj(x, y1, y1_row0, y2, w, mod, mod_row, final_g=None):
    n_rows = x.shape[0]
    b0 = y1_row0 // TM
    const = lambda i: (0, 0)
    in_specs = [
        pl.BlockSpec((TM, D), lambda i: (i, 0)),
        pl.BlockSpec((TM, 512), lambda i: (b0 + i, 0)),
        pl.BlockSpec((TM, 512), lambda i: (i, 0)),
        pl.BlockSpec((D, D), const),
        pl.BlockSpec((2, D), lambda i: (0, 2)),
    ]
    args = [x, y1, y2, w, mod]
    if final_g is not None:
        in_specs.append(pl.BlockSpec((1, D), const))
        args.append(final_g)
    return pl.pallas_call(
        functools.partial(_out_kernel, mod_row=mod_row, final=final_g is not None),
        out_shape=jax.ShapeDtypeStruct((n_rows, D), F32),
        grid=(n_rows // TM,),
        in_specs=in_specs,
        out_specs=pl.BlockSpec((TM, D), lambda i: (i, 0)),
        compiler_params=_params(("parallel",)),
        name="out_proj",
    )(*args)


def _rope_tables():
    t = jnp.arange(S)
    row = (t // GRID_W).astype(F32)
    col = (t % GRID_W).astype(F32)
    quarter = HEAD // 4
    inv = ROPE_THETA ** (-jnp.arange(quarter, dtype=F32) / quarter)
    ang_r = row[:, None] * inv[None, :]
    ang_c = col[:, None] * inv[None, :]
    ang = jnp.concatenate([ang_r, ang_r, ang_c, ang_c], axis=-1)
    cos, sin = jnp.cos(ang), jnp.sin(ang)
    first = (jnp.arange(HEAD) % 32) < 16
    sa = jnp.where(first, -sin, 0.0)
    sb = jnp.where(first, 0.0, sin)
    pad = lambda a, v: jnp.tile(jnp.concatenate([a, jnp.full((L, HEAD), v, F32)], axis=0), (1, 2))
    return pad(cos, 1.0), pad(sa, 0.0), pad(sb, 0.0)


def kernel(x, c, ctx, c_ctx, ev_norm_g, ev_w_ada, ev_b_ada, ev_w_in, ev_a_sink, ev_b_lambda, ev_b_subln_g, ev_w_out, od_norm_g, od_w_ada, od_b_ada, od_w_in, od_c_q_norm_g, od_c_kv_norm_g, od_c_w_qb, od_c_w_kvb, od_d_rpb, od_w_out, final_norm_g):
    assert x.shape == (1, S, D) and ctx.shape == (1, L, D)
    assert ev_w_in.shape[0] == 1 and od_w_in.shape[0] == 1
    x0, xc0 = x[0], ctx[0]
    ct = jnp.stack([c[0], c_ctx], axis=1)
    tables = _rope_tables()
    qk_scale = HEAD ** -0.5

    mod = _ada(ct, ev_w_ada[0], ev_b_ada[0][None])
    w = ev_w_in[0]
    k0, k1, v0, v1 = w[:, 512:576], w[:, 576:640], w[:, 640:704], w[:, 704:768]
    w_e = jnp.concatenate([w[:, 0:512] * qk_scale, k0, k0, k1, k1, v0, v0, v1, v1,
                           w[:, 768:1280] * qk_scale, w[:, 1280:]], axis=1).astype(BF16)
    proj = _proj_even(x0, xc0, mod, ev_norm_g[0][None], w_e, tables)
    ya = _mixer_a(proj, ev_a_sink[0])
    diff_extra = (ev_b_lambda[0], ev_b_subln_g[0][None])
    diff = functools.partial(_flash, proj, diff_extra, n_half=2, dq=128, qcol=E_QB, kcol=E_KB, vcol=E_VB,
                             gcol=E_G + 512, n_heads=4)
    yb = diff(q_row0=0, n_q_rows=S, kv_row0=0, n_kv_rows=T, tq=FLASH_TQ, tk=FLASH_TK, unroll=FLASH_UNROLL,
              name="diff_attn")
    ybc = diff(q_row0=S, n_q_rows=L, kv_row0=S, n_kv_rows=L, tq=L, tk=L, unroll=1, name="diff_attn_ctx")
    w_out = ev_w_out[0].astype(BF16)
    x1 = _out_proj(x0, ya, 0, yb, w_out, mod, 0)
    xc1 = _out_proj(xc0, ya, S, ybc, w_out, mod, 1)

    mod = _ada(ct, od_w_ada[0], od_b_ada[0][None])
    w = od_w_in[0]
    w_o = jnp.concatenate([w[:, 0:384], w[:, 384:448], w[:, 384:448], w[:, 448:960] * qk_scale, w[:, 960:]],
                          axis=1).astype(BF16)
    wqb = od_c_w_qb[0].reshape(256, 4, MLA_DK)
    wqb = jnp.concatenate([wqb[:, :, :128].reshape(256, 512), wqb[:, :, 128:].reshape(256, 256)],
                          axis=1).astype(BF16)
    proj = _proj_odd(x1, xc1, mod, od_norm_g[0][None], w_o, od_c_q_norm_g[0][None], od_c_kv_norm_g[0][None],
                     wqb, od_c_w_kvb[0].astype(BF16), tables)
    ym = _flash(proj, (), n_half=1, dq=256, qcol=O_QM, kcol=O_KM, vcol=O_VM, gcol=O_G, n_heads=4,
                q_row0=0, n_q_rows=S, kv_row0=0, n_kv_rows=T, tq=FLASH_TQ, tk=FLASH_TK, unroll=FLASH_UNROLL,
                name="mla_attn")
    yd = _mixer_d(proj, _neighbourhood_bias(od_d_rpb[0]))
    out = _out_proj(x1, ym, 0, yd, od_w_out[0].astype(BF16), mod, 0, final_norm_g[None])
    return out[None]
```

```python
import functools
import math

import jax
import jax.numpy as jnp
from jax import lax
from jax.experimental import pallas as pl
from jax.experimental.pallas import tpu as pltpu

F32 = jnp.float32
BF16 = jnp.bfloat16

D = 1024
S = 16384
L = 256
T = S + L
GRID_W = 64
EPS = 1e-6
NEG = -1e30
ROPE_THETA = 10000.0
A_WINDOW = 128
LANES = 128
HEAD = 64
MLA_DK = 192
LAM_INIT = 0.8 - 0.6 * math.exp(-0.3 * 0)

LOG2E = math.log2(math.e)
FLASH_TQ = 512
FLASH_TK = 1024
FLASH_UNROLL = 8
STALE_MAX_LIMIT = 30.0

TM = 256
N_LAT_TILES = S // TM
VMEM_LIMIT = 56 * 1024 * 1024

KV_SLAB = 256

R_QA, R_KA, R_VA, R_QB, R_KB, R_VB, R_G = 0, 512, 768, 1024, 1536, 2048, 2560
R_COLS = 3584
E_QA, E_KA, E_VA, E_KB, E_G = 0, 512, 768, 1024, 1536
E_COLS = 2560
O_KM, O_QD, O_KD, O_VD, O_G = 0, 1024, 1536, 2048, 2560
O_COLS = 3584


def _params(sem):
    return pltpu.CompilerParams(dimension_semantics=sem, vmem_limit_bytes=VMEM_LIMIT)


def _dot(a, b):
    return jnp.dot(a, b, preferred_element_type=F32)


def _dot_nt(a, b):
    return lax.dot_general(a, b, (((1,), (1,)), ((), ())), preferred_element_type=F32)


def _silu(z):
    return z / (1.0 + jnp.exp(-z))


def _rms(z):
    return z * lax.rsqrt(jnp.mean(z * z, axis=-1, keepdims=True) + EPS)


def _ada_kernel(ct_ref, w_ref, b_ref, o_ref):
    sc = _silu(ct_ref[...])
    w = w_ref[...]
    r0 = jnp.sum(w * sc[:, 0:1], axis=0, keepdims=True)
    r1 = jnp.sum(w * sc[:, 1:2], axis=0, keepdims=True)
    o_ref[...] = jnp.concatenate([r0, r1], axis=0) + b_ref[...]


def _ada(ct, w, b):
    tn = 768
    return pl.pallas_call(
        _ada_kernel,
        out_shape=jax.ShapeDtypeStruct((2, 3 * D), F32),
        grid=(3 * D // tn,),
        in_specs=[
            pl.BlockSpec((D, 2), lambda n: (0, 0)),
            pl.BlockSpec((D, tn), lambda n: (0, n)),
            pl.BlockSpec((1, tn), lambda n: (0, n)),
        ],
        out_specs=pl.BlockSpec((2, tn), lambda n: (0, n)),
        compiler_params=_params(("parallel",)),
        name="ada_mod",
    )(ct, w, b)


def _modulated(x_ref, ctx_ref, mod_ref, ng_ref):
    is_lat = pl.program_id(0) < N_LAT_TILES
    xin = jnp.where(is_lat, x_ref[...], ctx_ref[...])
    mod = mod_ref[...]
    modr = jnp.where(is_lat, mod[0:1], mod[1:2])
    shift, scale = modr[:, :D], modr[:, D:2 * D]
    return (_rms(xin) * ng_ref[...] * (1.0 + scale) + shift).astype(BF16)


def _rope_fn(cos_ref, sa_ref, sb_ref):
    cos, sa, sb = cos_ref[...], sa_ref[...], sb_ref[...]

    def rope(z):
        return z * cos + pltpu.roll(z, LANES - 16, 1) * sa + pltpu.roll(z, 16, 1) * sb

    return rope


def _tr(z):
    return z.T.astype(BF16)


def _proj_even_kernel(x_ref, ctx_ref, mod_ref, ng_ref, w_ref, cos_ref, sa_ref, sb_ref, o_ref, qt_ref, vt_ref):
    h = _modulated(x_ref, ctx_ref, mod_ref, ng_ref)
    res = _dot(h, w_ref[...])
    rope = _rope_fn(cos_ref, sa_ref, sb_ref)
    for j in range((R_VA - R_QA) // LANES):
        o_ref[:, E_QA + j * LANES:E_QA + (j + 1) * LANES] = rope(
            res[:, R_QA + j * LANES:R_QA + (j + 1) * LANES]).astype(BF16)
    o_ref[:, E_VA:E_KB] = res[:, R_VA:R_QB].astype(BF16)
    for j in range(4):
        sl = slice(j * LANES, (j + 1) * LANES)
        qt_ref[sl, :] = _tr(rope(res[:, R_QB + j * LANES:R_QB + (j + 1) * LANES]) * LOG2E)
        o_ref[:, E_KB + j * LANES:E_KB + (j + 1) * LANES] = rope(
            res[:, R_KB + j * LANES:R_KB + (j + 1) * LANES]).astype(BF16)
        vt_ref[0, sl, :] = _tr(res[:, R_VB + j * LANES:R_VB + (j + 1) * LANES])
    o_ref[:, E_G:] = _silu(res[:, R_G:]).astype(BF16)


def _proj_odd_kernel(x_ref, ctx_ref, mod_ref, ng_ref, w_ref, qg_ref, kvg_ref, wqb_ref, wkvb_ref,
                     cos_ref, sa_ref, sb_ref, o_ref, qt_ref, vt_ref):
    h = _modulated(x_ref, ctx_ref, mod_ref, ng_ref)
    res = _dot(h, w_ref[...])
    rope = _rope_fn(cos_ref, sa_ref, sb_ref)
    qn = (_rms(res[:, 0:256]) * qg_ref[...]).astype(BF16)
    kvn = (_rms(res[:, 256:384]) * kvg_ref[...]).astype(BF16)
    q = _dot(qn, wqb_ref[...]) * (MLA_DK ** -0.5 * LOG2E)
    kv = _dot(kvn, wkvb_ref[...])
    kpe = rope(res[:, 384:512]).astype(BF16)
    pes = (rope(q[:, 512:640]), rope(q[:, 640:768]))
    left = lax.broadcasted_iota(jnp.int32, (TM, LANES), 1) < HEAD
    for hd in range(4):
        pe = pes[hd // 2]
        pem = jnp.where(left if hd % 2 == 0 else jnp.logical_not(left), pe, 0.0)
        qt_ref[256 * hd:256 * hd + 128, :] = _tr(q[:, 128 * hd:128 * hd + 128])
        qt_ref[256 * hd + 128:256 * hd + 256, :] = _tr(pem)
        c0 = O_KM + 256 * hd
        o_ref[:, c0:c0 + 128] = kv[:, 256 * hd:256 * hd + 128].astype(BF16)
        o_ref[:, c0 + 128:c0 + 256] = kpe
        vt_ref[0, 128 * hd:128 * hd + 128, :] = _tr(kv[:, 256 * hd + 128:256 * hd + 256])
    o_ref[:, O_QD:O_G] = res[:, 512:2048].astype(BF16)
    o_ref[:, O_G:] = _silu(res[:, 2048:3072]).astype(BF16)


def _row_specs(n_w_cols):
    const = lambda i: (0, 0)
    return [
        pl.BlockSpec((TM, D), lambda i: (jnp.minimum(i, N_LAT_TILES - 1), 0)),
        pl.BlockSpec((L, D), const),
        pl.BlockSpec((2, 3 * D), const),
        pl.BlockSpec((1, D), const),
        pl.BlockSpec((D, n_w_cols), const),
    ]


def _table_specs():
    return [pl.BlockSpec((TM, LANES), lambda i: (i, 0))] * 3


def _proj_outs(n_cols, n_qt_rows):
    shapes = (jax.ShapeDtypeStruct((T, n_cols), BF16),
              jax.ShapeDtypeStruct((n_qt_rows, T), BF16),
              jax.ShapeDtypeStruct((T // KV_SLAB, 512, KV_SLAB), BF16))
    specs = (pl.BlockSpec((TM, n_cols), lambda i: (i, 0)),
             pl.BlockSpec((n_qt_rows, TM), lambda i: (0, i)),
             pl.BlockSpec((1, 512, KV_SLAB), lambda i: (i, 0, 0)))
    return shapes, specs


def _proj_even(x, ctx, mod, ng, w, tables):
    shapes, specs = _proj_outs(E_COLS, 512)
    return pl.pallas_call(
        _proj_even_kernel,
        out_shape=shapes,
        grid=(T // TM,),
        in_specs=_row_specs(R_COLS) + _table_specs(),
        out_specs=specs,
        compiler_params=_params(("parallel",)),
        name="proj_even",
    )(x, ctx, mod, ng, w, *tables)


def _proj_odd(x, ctx, mod, ng, w, qg, kvg, wqb, wkvb, tables):
    const = lambda i: (0, 0)
    shapes, specs = _proj_outs(O_COLS, 1024)
    return pl.pallas_call(
        _proj_odd_kernel,
        out_shape=shapes,
        grid=(T // TM,),
        in_specs=_row_specs(3072) + [
            pl.BlockSpec((1, 256), const),
            pl.BlockSpec((1, 128), const),
            pl.BlockSpec((256, 768), const),
            pl.BlockSpec((128, 1024), const),
        ] + _table_specs(),
        out_specs=specs,
        compiler_params=_params(("parallel",)),
        name="proj_odd",
    )(x, ctx, mod, ng, w, qg, kvg, wqb, wkvb, *tables)


def _flash_kernel(*refs, n_half, n_chunks, tk, tail, unroll):
    if n_half == 2:
        qt_ref, k_ref, vt_ref, g_ref, bl_ref, sg_ref, o_ref = refs
    else:
        qt_ref, k_ref, vt_ref, g_ref, o_ref = refs
    qt = qt_ref[...]
    tq = qt.shape[1]
    if n_half == 2:
        z = jnp.zeros((HEAD, tq), BF16)
        qs = (jnp.concatenate([qt[:HEAD], z], axis=0), jnp.concatenate([z, qt[HEAD:]], axis=0))
    else:
        qs = (qt,)

    def update(r0, n, carry, exact_max):
        k = k_ref[pl.ds(r0, n), :]
        s0 = r0 // KV_SLAB
        vt = jnp.concatenate([vt_ref[s0 + j] for j in range(n // KV_SLAB)], axis=1)
        new = []
        for t in range(n_half):
            m_prev, l_prev, acc, over = carry[t]
            st = _dot(k, qs[t])
            m_new = jnp.maximum(m_prev, jnp.max(st, axis=0, keepdims=True))
            alpha = jnp.exp2(m_prev - m_new)
            if exact_max:
                pt = jnp.exp2(st - m_new)
                l_new = alpha * l_prev + jnp.sum(pt, axis=0, keepdims=True)
                acc = alpha * acc + _dot(vt, pt.astype(BF16))
            else:
                pt = jnp.exp2(st - m_prev)
                l_new = alpha * (l_prev + jnp.sum(pt, axis=0, keepdims=True))
                acc = alpha * (acc + _dot(vt, pt.astype(BF16)))
                over = jnp.maximum(over, m_new - m_prev)
            new.append((m_new, l_new, acc, over))
        return tuple(new)

    carry = tuple((jnp.full((1, tq), NEG, F32), jnp.zeros((1, tq), F32), jnp.zeros((LANES, tq), F32),
                   jnp.zeros((1, tq), F32)) for _ in range(n_half))
    if tail:
        carry = update(n_chunks * tk, tail, carry, True)
    if n_chunks == 1:
        carry = update(0, tk, carry, not tail)
    elif n_chunks > 1:
        def walk(exact_max, unroll_):
            return lambda cr: lax.fori_loop(
                0, n_chunks, lambda c, cr_: update(pl.multiple_of(c * tk, tk), tk, cr_, exact_max), cr,
                unroll=unroll_)

        seeded = carry
        carry = walk(False, unroll)(seeded)
        worst = jnp.max(jnp.concatenate([cr[3] for cr in carry], axis=1))
        carry = lax.cond(worst > STALE_MAX_LIMIT, lambda _: walk(True, 1)(seeded), lambda cr: cr, carry)
    if n_half == 2:
        bl = bl_ref[...]
        lam = (jnp.exp(jnp.sum(bl[0:1] * bl[1:2], axis=1, keepdims=True))
               - jnp.exp(jnp.sum(bl[2:3] * bl[3:4], axis=1, keepdims=True)) + LAM_INIT)
        ot = carry[0][2] / carry[0][1] - lam * (carry[1][2] / carry[1][1])
        o = _rms(ot.T) * sg_ref[...] * (1.0 - LAM_INIT)
    else:
        o = (carry[0][2] / carry[0][1]).T
    o_ref[...] = (o * g_ref[...].astype(F32)).astype(BF16)


def _flash(qt, proj, vt, extra, *, n_half, dq, kcol, vrow, gcol, n_heads, q_row0, n_q_rows, kv_row0, n_kv_rows,
           tq, tk, unroll, name):
    nq, n_chunks = n_q_rows // tq, n_kv_rows // tk
    tail = n_kv_rows - n_chunks * tk
    qb0, kb0 = q_row0 // tq, kv_row0 // n_kv_rows
    in_specs = [
        pl.BlockSpec((dq, tq), lambda h, i: (h, qb0 + i)),
        pl.BlockSpec((n_kv_rows, dq), lambda h, i: (kb0, kcol // dq + h), pipeline_mode=pl.Buffered(1)),
        pl.BlockSpec((n_kv_rows // KV_SLAB, LANES, KV_SLAB), lambda h, i: (kb0, vrow // LANES + h, 0),
                     pipeline_mode=pl.Buffered(1)),
        pl.BlockSpec((tq, LANES), lambda h, i: (qb0 + i, gcol // LANES + h)),
    ]
    args = [qt, proj, vt, proj]
    if n_half == 2:
        in_specs += [pl.BlockSpec((4, HEAD), lambda h, i: (0, 0)),
                     pl.BlockSpec((1, LANES), lambda h, i: (0, 0))]
        args += list(extra)
    return pl.pallas_call(
        functools.partial(_flash_kernel, n_half=n_half, n_chunks=n_chunks, tk=tk, tail=tail, unroll=unroll),
        out_shape=jax.ShapeDtypeStruct((n_q_rows, n_heads * LANES), BF16),
        grid=(n_heads, nq),
        in_specs=in_specs,
        out_specs=pl.BlockSpec((tq, LANES), lambda h, i: (i, h)),
        compiler_params=_params(("parallel", "parallel")),
        name=name,
    )(*args)


def _local_blocks(i):
    last = T // TM - 1
    return jnp.maximum(i - 1, 0), i, jnp.minimum(i + 1, last), last


def _head_pair_merge(outs, g_ref, o_ref):
    left = lax.broadcasted_iota(jnp.int32, (TM, LANES), 1) < HEAD
    for sl in range(len(outs) // 2):
        o = jnp.where(left, outs[2 * sl], outs[2 * sl + 1])
        c0 = sl * LANES
        o_ref[:, c0:c0 + LANES] = (o * g_ref[:, c0:c0 + LANES].astype(F32)).astype(BF16)


def _masked_heads(q_ref, n_heads):
    left = lax.broadcasted_iota(jnp.int32, (TM, LANES), 1) < HEAD
    out = []
    for h in range(n_heads):
        qf = q_ref[:, (h // 2) * LANES:(h // 2 + 1) * LANES].astype(F32)
        out.append(jnp.where(left if h % 2 == 0 else jnp.logical_not(left), qf, 0.0).astype(BF16))
    return out


def _mixer_a_kernel(sink_ref, q_ref, k0_ref, k1_ref, k2_ref, kc_ref, v0_ref, v1_ref, v2_ref, vc_ref, g_ref, o_ref):
    i = pl.program_id(0)
    n_loc = 3 * TM
    k_all = jnp.concatenate([k0_ref[...], k1_ref[...], k2_ref[...], kc_ref[...]], axis=0)
    v_all = jnp.concatenate([v0_ref[...], v1_ref[...], v2_ref[...], vc_ref[...]], axis=0)
    qpos = i * TM + lax.broadcasted_iota(jnp.int32, (TM, n_loc + L), 0)
    col = lax.broadcasted_iota(jnp.int32, (TM, n_loc + L), 1)
    kpos = (i - 1) * TM + col
    in_band = jnp.where(jnp.abs(qpos - kpos) <= A_WINDOW, 1, 0) * jnp.where(kpos >= 0, 1, 0) \
        * jnp.where(kpos < S, 1, 0) * jnp.where(qpos < S, 1, 0)
    valid = jnp.where(col >= n_loc, 1, in_band) > 0
    qs = _masked_heads(q_ref, 8)
    outs = []
    for hq in range(8):
        kh = hq // 4
        s = _dot_nt(qs[hq], k_all[:, kh * LANES:(kh + 1) * LANES])
        s = jnp.where(valid, s, NEG)
        snk = sink_ref[hq]
        m = jnp.maximum(jnp.max(s, axis=1, keepdims=True), snk)
        p = jnp.exp(s - m)
        denom = jnp.sum(p, axis=1, keepdims=True) + jnp.exp(snk - m)
        o = _dot(p.astype(BF16), v_all[:, kh * LANES:(kh + 1) * LANES])
        outs.append(o / denom)
    _head_pair_merge(outs, g_ref, o_ref)


def _mixer_a(proj, sink):
    nt = T // TM

    def kv_spec(which, col_block):
        return pl.BlockSpec((TM, 256), lambda i: (_local_blocks(i)[which], col_block))

    return pl.pallas_call(
        _mixer_a_kernel,
        out_shape=jax.ShapeDtypeStruct((T, 512), BF16),
        grid=(nt,),
        in_specs=[pl.BlockSpec(memory_space=pltpu.SMEM),
                  pl.BlockSpec((TM, 512), lambda i: (i, E_QA // 512))]
        + [kv_spec(w, E_KA // 256) for w in range(4)]
        + [kv_spec(w, E_VA // 256) for w in range(4)]
        + [pl.BlockSpec((TM, 512), lambda i: (i, E_G // 512))],
        out_specs=pl.BlockSpec((TM, 512), lambda i: (i, 0)),
        compiler_params=_params(("parallel",)),
        name="mixer_a",
    )(sink, proj, *([proj] * 8), proj)


def _mixer_d_kernel(q_ref, k0_ref, k1_ref, k2_ref, kc_ref, v0_ref, v1_ref, v2_ref, vc_ref, bias_ref, g_ref, o_ref):
    n_loc = 3 * TM
    k_all = jnp.concatenate([k0_ref[...], k1_ref[...], k2_ref[...], kc_ref[...]], axis=0)
    v_all = jnp.concatenate([v0_ref[...], v1_ref[...], v2_ref[...], vc_ref[...]], axis=0)
    qs = _masked_heads(q_ref, 8)
    outs = []
    for h in range(8):
        sl = h // 2
        s = _dot_nt(qs[h], k_all[:, sl * LANES:(sl + 1) * LANES])
        s = jnp.concatenate([s[:, :n_loc] + bias_ref[0, h], s[:, n_loc:]], axis=1)
        m = jnp.max(s, axis=1, keepdims=True)
        p = jnp.exp(s - m)
        denom = jnp.sum(p, axis=1, keepdims=True)
        o = _dot(p.astype(BF16), v_all[:, sl * LANES:(sl + 1) * LANES])
        outs.append(o / denom)
    _head_pair_merge(outs, g_ref, o_ref)


def _mixer_d(proj, bias):
    nt = S // TM

    def kv_spec(which, col_block):
        return pl.BlockSpec((TM, 512), lambda i: (_local_blocks(i)[which], col_block))

    def variant(i):
        return jnp.where(i == 0, 0, jnp.where(i == nt - 1, 2, 1))

    return pl.pallas_call(
        _mixer_d_kernel,
        out_shape=jax.ShapeDtypeStruct((S, 512), BF16),
        grid=(nt,),
        in_specs=[pl.BlockSpec((TM, 512), lambda i: (i, O_QD // 512))]
        + [kv_spec(w, O_KD // 512) for w in range(4)]
        + [kv_spec(w, O_VD // 512) for w in range(4)]
        + [pl.BlockSpec((1, 8, TM, 3 * TM), lambda i: (variant(i), 0, 0, 0)),
           pl.BlockSpec((TM, 512), lambda i: (i, O_G // 512 + 1))],
        out_specs=pl.BlockSpec((TM, 512), lambda i: (i, 0)),
        compiler_params=_params(("parallel",)),
        name="mixer_d",
    )(proj, *([proj] * 8), bias, proj)


D_ROWS, D_COLS = 8, 16
RPB_R, RPB_C = 2 * D_ROWS - 1, 2 * D_COLS - 1
Q_GRID_ROWS = TM // GRID_W
K_GRID_ROWS = 3 * Q_GRID_ROWS


def _bias_kernel(rpb_ref, o_ref):
    base = pl.program_id(0) * (RPB_R * RPB_C)
    qc = lax.broadcasted_iota(jnp.int32, (GRID_W, LANES), 0)
    lane = lax.broadcasted_iota(jnp.int32, (GRID_W, LANES), 1)
    right = lane >= GRID_W
    kc = jnp.where(right, lane - GRID_W, lane)
    cs = jnp.clip(qc - D_COLS // 2, 0, GRID_W - D_COLS)
    col_ok = jnp.where(kc >= cs, 1, 0) * jnp.where(kc < cs + D_COLS, 1, 0)
    dc = jnp.where(col_ok > 0, kc - qc + (D_COLS - 1), -1)
    dc_l = jnp.where(right, -1, dc)
    dc_r = jnp.where(right, dc, -1)
    neg = jnp.full((GRID_W, LANES), NEG, F32)
    pairs = []
    for d in range(RPB_R - 1):
        t2 = neg
        for j in range(RPB_C):
            t2 = jnp.where(dc_l == j, rpb_ref[base + d * RPB_C + j], t2)
            t2 = jnp.where(dc_r == j, rpb_ref[base + (d + 1) * RPB_C + j], t2)
        pairs.append(t2)
    row_ok = (lambda ri, kr: kr >= Q_GRID_ROWS,
              lambda ri, kr: 0 <= kr - ri < D_ROWS,
              lambda ri, kr: kr < D_ROWS)
    for var in range(3):
        for ri in range(Q_GRID_ROWS):
            for a in range(K_GRID_ROWS // 2):
                blk = pairs[2 * a - ri + 3]
                ok_l, ok_r = row_ok[var](ri, 2 * a), row_ok[var](ri, 2 * a + 1)
                if not ok_l and not ok_r:
                    blk = neg
                elif not ok_l:
                    blk = jnp.where(right, blk, NEG)
                elif not ok_r:
                    blk = jnp.where(right, NEG, blk)
                o_ref[var, 0, ri * GRID_W:(ri + 1) * GRID_W, a * LANES:(a + 1) * LANES] = blk


def _neighbourhood_bias(rpb):
    n_heads = rpb.shape[0]
    return pl.pallas_call(
        _bias_kernel,
        out_shape=jax.ShapeDtypeStruct((3, n_heads, TM, 3 * TM), F32),
        grid=(n_heads,),
        in_specs=[pl.BlockSpec(memory_space=pltpu.SMEM)],
        out_specs=pl.BlockSpec((3, 1, TM, 3 * TM), lambda h: (0, h, 0, 0)),
        compiler_params=_params(("parallel",)),
        name="nbr_bias",
    )(rpb.reshape(-1))


def _out_kernel(*refs, mod_row, final):
    if final:
        x_ref, y1_ref, y2_ref, w_ref, gate_ref, fg_ref, o_ref = refs
    else:
        x_ref, y1_ref, y2_ref, w_ref, gate_ref, o_ref = refs
    y = jnp.concatenate([y1_ref[...], y2_ref[...]], axis=1)
    xn = x_ref[...] + gate_ref[mod_row:mod_row + 1, :] * _dot(y, w_ref[...])
    if final:
        xn = _rms(xn) * fg_ref[...]
    o_ref[...] = xn


def _out_proj(x, y1, y1_row0, y2, w, mod, mod_row, final_g=None):
    n_rows = x.shape[0]
    b0 = y1_row0 // TM
    const = lambda i: (0, 0)
    in_specs = [
        pl.BlockSpec((TM, D), lambda i: (i, 0)),
        pl.BlockSpec((TM, 512), lambda i: (b0 + i, 0)),
        pl.BlockSpec((TM, 512), lambda i: (i, 0)),
        pl.BlockSpec((D, D), const),
        pl.BlockSpec((2, D), lambda i: (0, 2)),
    ]
    args = [x, y1, y2, w, mod]
    if final_g is not None:
        in_specs.append(pl.BlockSpec((1, D), const))
        args.append(final_g)
    return pl.pallas_call(
        functools.partial(_out_kernel, mod_row=mod_row, final=final_g is not None),
        out_shape=jax.ShapeDtypeStruct((n_rows, D), F32),
        grid=(n_rows // TM,),
        in_specs=in_specs,
        out_specs=pl.BlockSpec((TM, D), lambda i: (i, 0)),
        compiler_params=_params(("parallel",)),
        name="out_proj",
    )(*args)


def _rope_tables():
    t = jnp.arange(S)
    row = (t // GRID_W).astype(F32)
    col = (t % GRID_W).astype(F32)
    quarter = HEAD // 4
    inv = ROPE_THETA ** (-jnp.arange(quarter, dtype=F32) / quarter)
    ang_r = row[:, None] * inv[None, :]
    ang_c = col[:, None] * inv[None, :]
    ang = jnp.concatenate([ang_r, ang_r, ang_c, ang_c], axis=-1)
    cos, sin = jnp.cos(ang), jnp.sin(ang)
    first = (jnp.arange(HEAD) % 32) < 16
    sa = jnp.where(first, -sin, 0.0)
    sb = jnp.where(first, 0.0, sin)
    pad = lambda a, v: jnp.tile(jnp.concatenate([a, jnp.full((L, HEAD), v, F32)], axis=0), (1, 2))
    return pad(cos, 1.0), pad(sa, 0.0), pad(sb, 0.0)


def kernel(x, c, ctx, c_ctx, ev_norm_g, ev_w_ada, ev_b_ada, ev_w_in, ev_a_sink, ev_b_lambda, ev_b_subln_g, ev_w_out, od_norm_g, od_w_ada, od_b_ada, od_w_in, od_c_q_norm_g, od_c_kv_norm_g, od_c_w_qb, od_c_w_kvb, od_d_rpb, od_w_out, final_norm_g):
    assert x.shape == (1, S, D) and ctx.shape == (1, L, D)
    assert ev_w_in.shape[0] == 1 and od_w_in.shape[0] == 1
    x0, xc0 = x[0], ctx[0]
    ct = jnp.stack([c[0], c_ctx], axis=1)
    tables = _rope_tables()
    qk_scale = HEAD ** -0.5

    mod = _ada(ct, ev_w_ada[0], ev_b_ada[0][None])
    w = ev_w_in[0]
    k0, k1, v0, v1 = w[:, 512:576], w[:, 576:640], w[:, 640:704], w[:, 704:768]
    w_e = jnp.concatenate([w[:, 0:512] * qk_scale, k0, k0, k1, k1, v0, v0, v1, v1,
                           w[:, 768:1280] * qk_scale, w[:, 1280:]], axis=1).astype(BF16)
    proj, qbt, vbt = _proj_even(x0, xc0, mod, ev_norm_g[0][None], w_e, tables)
    ya = _mixer_a(proj, ev_a_sink[0])
    diff_extra = (ev_b_lambda[0], ev_b_subln_g[0][None])
    diff = functools.partial(_flash, qbt, proj, vbt, diff_extra, n_half=2, dq=128, kcol=E_KB, vrow=0,
                             gcol=E_G + 512, n_heads=4)
    yb = diff(q_row0=0, n_q_rows=S, kv_row0=0, n_kv_rows=T, tq=FLASH_TQ, tk=FLASH_TK, unroll=FLASH_UNROLL,
              name="diff_attn")
    ybc = diff(q_row0=S, n_q_rows=L, kv_row0=S, n_kv_rows=L, tq=L, tk=L, unroll=1, name="diff_attn_ctx")
    w_out = ev_w_out[0].astype(BF16)
    x1 = _out_proj(x0, ya, 0, yb, w_out, mod, 0)
    xc1 = _out_proj(xc0, ya, S, ybc, w_out, mod, 1)

    mod = _ada(ct, od_w_ada[0], od_b_ada[0][None])
    w = od_w_in[0]
    w_o = jnp.concatenate([w[:, 0:384], w[:, 384:448], w[:, 384:448], w[:, 448:960] * qk_scale, w[:, 960:]],
                          axis=1).astype(BF16)
    wqb = od_c_w_qb[0].reshape(256, 4, MLA_DK)
    wqb = jnp.concatenate([wqb[:, :, :128].reshape(256, 512), wqb[:, :, 128:].reshape(256, 256)],
                          axis=1).astype(BF16)
    proj, qmt, vmt = _proj_odd(x1, xc1, mod, od_norm_g[0][None], w_o, od_c_q_norm_g[0][None],
                               od_c_kv_norm_g[0][None], wqb, od_c_w_kvb[0].astype(BF16), tables)
    ym = _flash(qmt, proj, vmt, (), n_half=1, dq=256, kcol=O_KM, vrow=0, gcol=O_G, n_heads=4,
                q_row0=0, n_q_rows=S, kv_row0=0, n_kv_rows=T, tq=FLASH_TQ, tk=FLASH_TK, unroll=FLASH_UNROLL,
                name="mla_attn")
    yd = _mixer_d(proj, _neighbourhood_bias(od_d_rpb[0]))
    out = _out_proj(x1, ym, 0, yd, od_w_out[0].astype(BF16), mod, 0, final_norm_g[None])
    return out[None]
```

```python
import functools
import math

import jax
import jax.numpy as jnp
from jax import lax
from jax.experimental import pallas as pl
from jax.experimental.pallas import tpu as pltpu

F32 = jnp.float32
BF16 = jnp.bfloat16

D = 1024
S = 16384
L = 256
T = S + L
GRID_W = 64
EPS = 1e-6
NEG = -1e30
ROPE_THETA = 10000.0
A_WINDOW = 128
LANES = 128
HEAD = 64
MLA_DK = 192
LAM_INIT = 0.8 - 0.6 * math.exp(-0.3 * 0)

LOG2E = math.log2(math.e)
FLASH_TQ = 512
FLASH_TK = 1024
FLASH_UNROLL = 16
STALE_MAX_LIMIT = 30.0

TM = 256
N_LAT_TILES = S // TM
CTX_TILE = S // TM
VMEM_LIMIT = 56 * 1024 * 1024
KV_SLAB = 256

R_QA, R_KA, R_VA, R_QB, R_KB, R_VB, R_G = 0, 512, 640, 768, 1280, 1792, 2304
R_COLS = 3328
E_G, E_KB, E_KA = 0, 1024, 1536
E_COLS = 1664
EQ_A, EQ_B, EQ_ROWS = 0, 512, 1024
EV_B, EV_A, EV_ROWS = 0, 512, 640
O_G, O_KM, O_KD = 0, 1024, 2048
O_COLS = 2560
OQ_M, OQ_D, OQ_ROWS = 0, 1024, 1536
OV_M, OV_D, OV_ROWS = 0, 512, 1024


def _params(sem):
    return pltpu.CompilerParams(dimension_semantics=sem, vmem_limit_bytes=VMEM_LIMIT)


def _dot(a, b):
    return jnp.dot(a, b, preferred_element_type=F32)


def _silu(z):
    return z / (1.0 + jnp.exp(-z))


def _rms(z):
    return z * lax.rsqrt(jnp.mean(z * z, axis=-1, keepdims=True) + EPS)


def _ada_kernel(ct_ref, w_ref, b_ref, o_ref):
    sc = _silu(ct_ref[...])
    w = w_ref[...]
    r0 = jnp.sum(w * sc[:, 0:1], axis=0, keepdims=True)
    r1 = jnp.sum(w * sc[:, 1:2], axis=0, keepdims=True)
    o_ref[...] = jnp.concatenate([r0, r1], axis=0) + b_ref[...]


def _ada(ct, w, b):
    tn = 768
    return pl.pallas_call(
        _ada_kernel,
        out_shape=jax.ShapeDtypeStruct((2, 3 * D), F32),
        grid=(3 * D // tn,),
        in_specs=[
            pl.BlockSpec((D, 2), lambda n: (0, 0)),
            pl.BlockSpec((D, tn), lambda n: (0, n)),
            pl.BlockSpec((1, tn), lambda n: (0, n)),
        ],
        out_specs=pl.BlockSpec((2, tn), lambda n: (0, n)),
        compiler_params=_params(("parallel",)),
        name="ada_mod",
    )(ct, w, b)


def _modulated(x_ref, ctx_ref, mod_ref, ng_ref):
    is_lat = pl.program_id(0) < N_LAT_TILES
    xin = jnp.where(is_lat, x_ref[...], ctx_ref[...])
    mod = mod_ref[...]
    modr = jnp.where(is_lat, mod[0:1], mod[1:2])
    shift, scale = modr[:, :D], modr[:, D:2 * D]
    return (_rms(xin) * ng_ref[...] * (1.0 + scale) + shift).astype(BF16)


def _rope_fn(cos_ref, sa_ref, sb_ref):
    cos, sa, sb = cos_ref[...], sa_ref[...], sb_ref[...]

    def rope(z):
        return z * cos + pltpu.roll(z, LANES - 16, 1) * sa + pltpu.roll(z, 16, 1) * sb

    return rope


def _tr(z):
    return z.T.astype(BF16)


def _slabs(lo, n):
    return [slice(lo + j * LANES, lo + (j + 1) * LANES) for j in range(n // LANES)]


def _proj_even_kernel(x_ref, ctx_ref, mod_ref, ng_ref, w_ref, cos_ref, sa_ref, sb_ref, o_ref, qt_ref, vt_ref):
    h = _modulated(x_ref, ctx_ref, mod_ref, ng_ref)
    res = _dot(h, w_ref[...])
    rope = _rope_fn(cos_ref, sa_ref, sb_ref)
    for src, dst in zip(_slabs(R_QA, 512) + _slabs(R_QB, 512), _slabs(EQ_A, 1024)):
        qt_ref[dst, :] = _tr(rope(res[:, src]) * LOG2E)
    for src, dst in zip(_slabs(R_KB, 512) + _slabs(R_KA, 128), _slabs(E_KB, 640)):
        o_ref[:, dst] = rope(res[:, src]).astype(BF16)
    for src, dst in zip(_slabs(R_VB, 512) + _slabs(R_VA, 128), _slabs(EV_B, 640)):
        vt_ref[0, dst, :] = _tr(res[:, src])
    o_ref[:, E_G:E_G + D] = _silu(res[:, R_G:]).astype(BF16)


def _proj_odd_kernel(x_ref, ctx_ref, mod_ref, ng_ref, w_ref, qg_ref, kvg_ref, wqb_ref, wkvb_ref,
                     cos_ref, sa_ref, sb_ref, o_ref, qt_ref, vt_ref):
    h = _modulated(x_ref, ctx_ref, mod_ref, ng_ref)
    res = _dot(h, w_ref[...])
    rope = _rope_fn(cos_ref, sa_ref, sb_ref)
    qn = (_rms(res[:, 0:256]) * qg_ref[...]).astype(BF16)
    kvn = (_rms(res[:, 256:384]) * kvg_ref[...]).astype(BF16)
    q = _dot(qn, wqb_ref[...]) * (MLA_DK ** -0.5 * LOG2E)
    kv = _dot(kvn, wkvb_ref[...])
    kpe = rope(res[:, 384:512]).astype(BF16)
    pes = (rope(q[:, 512:640]), rope(q[:, 640:768]))
    left = lax.broadcasted_iota(jnp.int32, (TM, LANES), 1) < HEAD
    for hd in range(4):
        pe = pes[hd // 2]
        pem = jnp.where(left if hd % 2 == 0 else jnp.logical_not(left), pe, 0.0)
        qt_ref[OQ_M + 256 * hd:OQ_M + 256 * hd + 128, :] = _tr(q[:, 128 * hd:128 * hd + 128])
        qt_ref[OQ_M + 256 * hd + 128:OQ_M + 256 * hd + 256, :] = _tr(pem)
        c0 = O_KM + 256 * hd
        o_ref[:, c0:c0 + 128] = kv[:, 256 * hd:256 * hd + 128].astype(BF16)
        o_ref[:, c0 + 128:c0 + 256] = kpe
        vt_ref[0, OV_M + 128 * hd:OV_M + 128 * hd + 128, :] = _tr(kv[:, 256 * hd + 128:256 * hd + 256])
    for src, dst in zip(_slabs(512, 512), _slabs(OQ_D, 512)):
        qt_ref[dst, :] = _tr(res[:, src] * LOG2E)
    o_ref[:, O_KD:O_KD + 512] = res[:, 1024:1536].astype(BF16)
    for src, dst in zip(_slabs(1536, 512), _slabs(OV_D, 512)):
        vt_ref[0, dst, :] = _tr(res[:, src])
    o_ref[:, O_G:O_G + D] = _silu(res[:, 2048:3072]).astype(BF16)


def _row_specs(n_w_cols):
    const = lambda i: (0, 0)
    return [
        pl.BlockSpec((TM, D), lambda i: (jnp.minimum(i, N_LAT_TILES - 1), 0)),
        pl.BlockSpec((L, D), const),
        pl.BlockSpec((2, 3 * D), const),
        pl.BlockSpec((1, D), const),
        pl.BlockSpec((D, n_w_cols), const),
    ]


def _table_specs():
    return [pl.BlockSpec((TM, LANES), lambda i: (i, 0))] * 3


def _proj_outs(n_cols, n_qt_rows, n_vt_rows):
    shapes = (jax.ShapeDtypeStruct((T, n_cols), BF16),
              jax.ShapeDtypeStruct((n_qt_rows, T), BF16),
              jax.ShapeDtypeStruct((T // KV_SLAB, n_vt_rows, KV_SLAB), BF16))
    specs = (pl.BlockSpec((TM, n_cols), lambda i: (i, 0)),
             pl.BlockSpec((n_qt_rows, TM), lambda i: (0, i)),
             pl.BlockSpec((1, n_vt_rows, KV_SLAB), lambda i: (i, 0, 0)))
    return shapes, specs


def _proj_even(x, ctx, mod, ng, w, tables):
    shapes, specs = _proj_outs(E_COLS, EQ_ROWS, EV_ROWS)
    return pl.pallas_call(
        _proj_even_kernel,
        out_shape=shapes,
        grid=(T // TM,),
        in_specs=_row_specs(R_COLS) + _table_specs(),
        out_specs=specs,
        compiler_params=_params(("parallel",)),
        name="proj_even",
    )(x, ctx, mod, ng, w, *tables)


def _proj_odd(x, ctx, mod, ng, w, qg, kvg, wqb, wkvb, tables):
    const = lambda i: (0, 0)
    shapes, specs = _proj_outs(O_COLS, OQ_ROWS, OV_ROWS)
    return pl.pallas_call(
        _proj_odd_kernel,
        out_shape=shapes,
        grid=(T // TM,),
        in_specs=_row_specs(3072) + [
            pl.BlockSpec((1, 256), const),
            pl.BlockSpec((1, 128), const),
            pl.BlockSpec((256, 768), const),
            pl.BlockSpec((128, 1024), const),
        ] + _table_specs(),
        out_specs=specs,
        compiler_params=_params(("parallel",)),
        name="proj_odd",
    )(x, ctx, mod, ng, w, qg, kvg, wqb, wkvb, *tables)


def _flash_kernel(*refs, n_half, n_chunks, tk, tail, unroll):
    if n_half == 2:
        qt_ref, k_ref, vt_ref, g_ref, bl_ref, sg_ref, o_ref = refs
    else:
        qt_ref, k_ref, vt_ref, g_ref, o_ref = refs
    qt = qt_ref[...]
    tq = qt.shape[1]
    if n_half == 2:
        z = jnp.zeros((HEAD, tq), BF16)
        qs = (jnp.concatenate([qt[:HEAD], z], axis=0), jnp.concatenate([z, qt[HEAD:]], axis=0))
    else:
        qs = (qt,)

    def update(r0, n, carry, exact_max):
        k = k_ref[pl.ds(r0, n), :]
        s0 = r0 // KV_SLAB
        vt = jnp.concatenate([vt_ref[s0 + j] for j in range(n // KV_SLAB)], axis=1)
        new = []
        for t in range(n_half):
            m_prev, l_prev, acc, over = carry[t]
            st = _dot(k, qs[t])
            m_new = jnp.maximum(m_prev, jnp.max(st, axis=0, keepdims=True))
            alpha = jnp.exp2(m_prev - m_new)
            if exact_max:
                pt = jnp.exp2(st - m_new)
                l_new = alpha * l_prev + jnp.sum(pt, axis=0, keepdims=True)
                acc = alpha * acc + _dot(vt, pt.astype(BF16))
            else:
                pt = jnp.exp2(st - m_prev)
                l_new = alpha * (l_prev + jnp.sum(pt, axis=0, keepdims=True))
                acc = alpha * (acc + _dot(vt, pt.astype(BF16)))
                over = jnp.maximum(over, m_new - m_prev)
            new.append((m_new, l_new, acc, over))
        return tuple(new)

    carry = tuple((jnp.full((1, tq), NEG, F32), jnp.zeros((1, tq), F32), jnp.zeros((LANES, tq), F32),
                   jnp.zeros((1, tq), F32)) for _ in range(n_half))
    if tail:
        carry = update(n_chunks * tk, tail, carry, True)
    if n_chunks == 1:
        carry = update(0, tk, carry, not tail)
    elif n_chunks > 1:
        def walk(exact_max, unroll_):
            return lambda cr: lax.fori_loop(
                0, n_chunks, lambda c, cr_: update(pl.multiple_of(c * tk, tk), tk, cr_, exact_max), cr,
                unroll=unroll_)

        seeded = carry
        carry = walk(False, unroll)(seeded)
        worst = jnp.max(jnp.concatenate([cr[3] for cr in carry], axis=1))
        carry = lax.cond(worst > STALE_MAX_LIMIT, lambda _: walk(True, 1)(seeded), lambda cr: cr, carry)
    if n_half == 2:
        bl = bl_ref[...]
        lam = (jnp.exp(jnp.sum(bl[0:1] * bl[1:2], axis=1, keepdims=True))
               - jnp.exp(jnp.sum(bl[2:3] * bl[3:4], axis=1, keepdims=True)) + LAM_INIT)
        ot = carry[0][2] / carry[0][1] - lam * (carry[1][2] / carry[1][1])
        o = _rms(ot.T) * sg_ref[...] * (1.0 - LAM_INIT)
    else:
        o = (carry[0][2] / carry[0][1]).T
    o_ref[...] = (o * g_ref[...].astype(F32)).astype(BF16)


def _flash(qt, proj, vt, extra, *, n_half, dq, qrow, kcol, vrow, gcol, n_heads, q_row0, n_q_rows, kv_row0,
           n_kv_rows, tq, tk, unroll, name):
    nq, n_chunks = n_q_rows // tq, n_kv_rows // tk
    tail = n_kv_rows - n_chunks * tk
    qb0, kb0 = q_row0 // tq, kv_row0 // n_kv_rows
    in_specs = [
        pl.BlockSpec((dq, tq), lambda h, i: (qrow // dq + h, qb0 + i)),
        pl.BlockSpec((n_kv_rows, dq), lambda h, i: (kb0, kcol // dq + h), pipeline_mode=pl.Buffered(1)),
        pl.BlockSpec((n_kv_rows // KV_SLAB, LANES, KV_SLAB), lambda h, i: (kb0, vrow // LANES + h, 0),
                     pipeline_mode=pl.Buffered(1)),
        pl.BlockSpec((tq, LANES), lambda h, i: (qb0 + i, gcol // LANES + h)),
    ]
    args = [qt, proj, vt, proj]
    if n_half == 2:
        in_specs += [pl.BlockSpec((4, HEAD), lambda h, i: (0, 0)),
                     pl.BlockSpec((1, LANES), lambda h, i: (0, 0))]
        args += list(extra)
    return pl.pallas_call(
        functools.partial(_flash_kernel, n_half=n_half, n_chunks=n_chunks, tk=tk, tail=tail, unroll=unroll),
        out_shape=jax.ShapeDtypeStruct((n_q_rows, n_heads * LANES), BF16),
        grid=(n_heads, nq),
        in_specs=in_specs,
        out_specs=pl.BlockSpec((tq, LANES), lambda h, i: (i, h)),
        compiler_params=_params(("parallel", "parallel")),
        name=name,
    )(*args)


def _zero_padded(qh, upper):
    z = jnp.zeros_like(qh)
    return jnp.concatenate([qh, z] if upper else [z, qh], axis=0)


def _store_head_pair(o_ref, g_ref, slab, top, bottom):
    o = jnp.concatenate([top, bottom], axis=0).T
    c = slice(slab * LANES, (slab + 1) * LANES)
    o_ref[:, c] = (o * g_ref[:, c].astype(F32)).astype(BF16)


def _mixer_a_kernel(sink_ref, qt_ref, k0_ref, k1_ref, k2_ref, k3_ref, kc_ref, v0_ref, v1_ref, v2_ref, vc_ref,
                    g_ref, o_ref):
    i = pl.program_id(0)
    n_loc = TM + 2 * A_WINDOW
    n_keys = n_loc + L
    k_all = jnp.concatenate([k0_ref[...], k1_ref[...], k2_ref[...], k3_ref[...], kc_ref[...]], axis=0)
    vt_all = jnp.concatenate([v0_ref[0][:, TM - A_WINDOW:], v1_ref[0], v2_ref[0][:, :A_WINDOW], vc_ref[0]],
                             axis=1)
    kl = lax.broadcasted_iota(jnp.int32, (n_keys, TM), 0)
    ql = lax.broadcasted_iota(jnp.int32, (n_keys, TM), 1)
    kpos = i * TM - A_WINDOW + kl
    dist = kl - ql
    ok = (jnp.where(dist >= 0, 1, 0) * jnp.where(dist <= 2 * A_WINDOW, 1, 0) * jnp.where(kpos >= 0, 1, 0)
          * jnp.where(kpos < S, 1, 0) * jnp.where(i < N_LAT_TILES, 1, 0))
    mask = jnp.where(jnp.where(kl >= n_loc, 1, ok) > 0, 0.0, NEG)
    qt = qt_ref[...]
    for kh in range(2):
        heads = range(4 * kh, 4 * kh + 4)
        w = jnp.concatenate([_zero_padded(qt[HEAD * hq:HEAD * (hq + 1)], kh == 0) for hq in heads], axis=1)
        st = _dot(k_all, w) + jnp.concatenate([mask] * 4, axis=1)
        snk = jnp.concatenate([jnp.full((1, TM), sink_ref[hq] * LOG2E, F32) for hq in heads], axis=1)
        m = jnp.maximum(jnp.max(st, axis=0, keepdims=True), snk)
        pt = jnp.exp2(st - m)
        denom = jnp.sum(pt, axis=0, keepdims=True) + jnp.exp2(snk - m)
        ot = _dot(vt_all, pt.astype(BF16)) * (1.0 / denom)
        ot = ot[HEAD * kh:HEAD * (kh + 1)]
        for pair in range(2):
            _store_head_pair(o_ref, g_ref, 2 * kh + pair, ot[:, 2 * pair * TM:(2 * pair + 1) * TM],
                             ot[:, (2 * pair + 1) * TM:(2 * pair + 2) * TM])


def _mixer_a(qt, proj, vt, sink):
    nt = T // TM
    last_blk = T // A_WINDOW - 1

    def k_spec(off):
        return pl.BlockSpec((A_WINDOW, LANES),
                            lambda i: (jnp.clip(2 * i + off, 0, last_blk), E_KA // LANES))

    def v_spec(off):
        return pl.BlockSpec((1, LANES, KV_SLAB), lambda i: (jnp.clip(i + off, 0, nt - 1), EV_A // LANES, 0))

    return pl.pallas_call(
        _mixer_a_kernel,
        out_shape=jax.ShapeDtypeStruct((T, 512), BF16),
        grid=(nt,),
        in_specs=[pl.BlockSpec(memory_space=pltpu.SMEM),
                  pl.BlockSpec((512, TM), lambda i: (EQ_A // 512, i))]
        + [k_spec(off) for off in (-1, 0, 1, 2)]
        + [pl.BlockSpec((L, LANES), lambda i: (CTX_TILE, E_KA // LANES))]
        + [v_spec(off) for off in (-1, 0, 1)]
        + [pl.BlockSpec((1, LANES, KV_SLAB), lambda i: (CTX_TILE, EV_A // LANES, 0)),
           pl.BlockSpec((TM, 512), lambda i: (i, E_G // 512))],
        out_specs=pl.BlockSpec((TM, 512), lambda i: (i, 0)),
        compiler_params=_params(("parallel",)),
        name="mixer_a",
    )(sink, qt, *([proj] * 5), *([vt] * 4), proj)


def _mixer_d_kernel(qt_ref, k0_ref, k1_ref, k2_ref, kc_ref, v0_ref, v1_ref, v2_ref, vc_ref, bias_ref, g_ref, o_ref):
    n_loc = 3 * TM
    k_all = jnp.concatenate([k0_ref[...], k1_ref[...], k2_ref[...], kc_ref[...]], axis=0)
    vt_all = jnp.concatenate([v0_ref[0], v1_ref[0], v2_ref[0], vc_ref[0]], axis=1)
    qt = qt_ref[...]
    for pair in range(4):
        rows = slice(pair * LANES, (pair + 1) * LANES)
        w = jnp.concatenate([_zero_padded(qt[rows][:HEAD], True), _zero_padded(qt[rows][HEAD:], False)], axis=1)
        st = _dot(k_all[:, rows], w)
        bias = jnp.concatenate([bias_ref[0, 2 * pair], bias_ref[0, 2 * pair + 1]], axis=1)
        st = jnp.concatenate([st[:n_loc] + bias, st[n_loc:]], axis=0)
        m = jnp.max(st, axis=0, keepdims=True)
        pt = jnp.exp2(st - m)
        denom = jnp.sum(pt, axis=0, keepdims=True)
        ot = _dot(vt_all[rows], pt.astype(BF16)) * (1.0 / denom)
        _store_head_pair(o_ref, g_ref, pair, ot[:HEAD, :TM], ot[HEAD:, TM:])


def _mixer_d(qt, proj, vt, bias):
    nt = S // TM

    def k_spec(off):
        return pl.BlockSpec((TM, 512), lambda i: (jnp.clip(i + off, 0, nt - 1), O_KD // 512))

    def v_spec(off):
        return pl.BlockSpec((1, 512, KV_SLAB), lambda i: (jnp.clip(i + off, 0, nt - 1), OV_D // 512, 0))

    def variant(i):
        return jnp.where(i == 0, 0, jnp.where(i == nt - 1, 2, 1))

    return pl.pallas_call(
        _mixer_d_kernel,
        out_shape=jax.ShapeDtypeStruct((S, 512), BF16),
        grid=(nt,),
        in_specs=[pl.BlockSpec((512, TM), lambda i: (OQ_D // 512, i))]
        + [k_spec(off) for off in (-1, 0, 1)]
        + [pl.BlockSpec((L, 512), lambda i: (CTX_TILE, O_KD // 512))]
        + [v_spec(off) for off in (-1, 0, 1)]
        + [pl.BlockSpec((1, 512, KV_SLAB), lambda i: (CTX_TILE, OV_D // 512, 0)),
           pl.BlockSpec((1, 8, 3 * TM, TM), lambda i: (variant(i), 0, 0, 0)),
           pl.BlockSpec((TM, 512), lambda i: (i, O_G // 512 + 1))],
        out_specs=pl.BlockSpec((TM, 512), lambda i: (i, 0)),
        compiler_params=_params(("parallel",)),
        name="mixer_d",
    )(qt, *([proj] * 4), *([vt] * 4), bias, proj)


D_ROWS, D_COLS = 8, 16
RPB_R, RPB_C = 2 * D_ROWS - 1, 2 * D_COLS - 1
Q_GRID_ROWS = TM // GRID_W
K_GRID_ROWS = 3 * Q_GRID_ROWS


def _bias_kernel(rpb_ref, o_ref):
    base = pl.program_id(0) * (RPB_R * RPB_C)
    kc = lax.broadcasted_iota(jnp.int32, (GRID_W, LANES), 0)
    lane = lax.broadcasted_iota(jnp.int32, (GRID_W, LANES), 1)
    right = lane >= GRID_W
    qc = jnp.where(right, lane - GRID_W, lane)
    cs = jnp.clip(qc - D_COLS // 2, 0, GRID_W - D_COLS)
    col_ok = jnp.where(kc >= cs, 1, 0) * jnp.where(kc < cs + D_COLS, 1, 0)
    dc = jnp.where(col_ok > 0, kc - qc + (D_COLS - 1), -1)
    dc_l = jnp.where(right, -1, dc)
    dc_r = jnp.where(right, dc, -1)
    neg = jnp.full((GRID_W, LANES), NEG, F32)
    pairs = []
    for d in range(RPB_R - 1):
        t2 = neg
        for j in range(RPB_C):
            t2 = jnp.where(dc_l == j, rpb_ref[base + (d + 1) * RPB_C + j] * LOG2E, t2)
            t2 = jnp.where(dc_r == j, rpb_ref[base + d * RPB_C + j] * LOG2E, t2)
        pairs.append(t2)
    row_ok = (lambda ri, kr: kr >= Q_GRID_ROWS,
              lambda ri, kr: 0 <= kr - ri < D_ROWS,
              lambda ri, kr: kr < D_ROWS)
    for var in range(3):
        for kr in range(K_GRID_ROWS):
            for b in range(Q_GRID_ROWS // 2):
                blk = pairs[kr - 2 * b + 2]
                ok_l, ok_r = row_ok[var](2 * b, kr), row_ok[var](2 * b + 1, kr)
                if not ok_l and not ok_r:
                    blk = neg
                elif not ok_l:
                    blk = jnp.where(right, blk, NEG)
                elif not ok_r:
                    blk = jnp.where(right, NEG, blk)
                o_ref[var, 0, kr * GRID_W:(kr + 1) * GRID_W, b * LANES:(b + 1) * LANES] = blk


def _neighbourhood_bias(rpb):
    n_heads = rpb.shape[0]
    return pl.pallas_call(
        _bias_kernel,
        out_shape=jax.ShapeDtypeStruct((3, n_heads, 3 * TM, TM), F32),
        grid=(n_heads,),
        in_specs=[pl.BlockSpec(memory_space=pltpu.SMEM)],
        out_specs=pl.BlockSpec((3, 1, 3 * TM, TM), lambda h: (0, h, 0, 0)),
        compiler_params=_params(("parallel",)),
        name="nbr_bias",
    )(rpb.reshape(-1))


def _out_kernel(*refs, mod_row, final):
    if final:
        x_ref, y1_ref, y2_ref, w_ref, gate_ref, fg_ref, o_ref = refs
    else:
        x_ref, y1_ref, y2_ref, w_ref, gate_ref, o_ref = refs
    y = jnp.concatenate([y1_ref[...], y2_ref[...]], axis=1)
    xn = x_ref[...] + gate_ref[mod_row:mod_row + 1, :] * _dot(y, w_ref[...])
    if final:
        xn = _rms(xn) * fg_ref[...]
    o_ref[...] = xn


def _out_proj(x, y1, y1_row0, y2, w, mod, mod_row, final_g=None):
    n_rows = x.shape[0]
    b0 = y1_row0 // TM
    const = lambda i: (0, 0)
    in_specs = [
        pl.BlockSpec((TM, D), lambda i: (i, 0)),
        pl.BlockSpec((TM, 512), lambda i: (b0 + i, 0)),
        pl.BlockSpec((TM, 512), lambda i: (i, 0)),
        pl.BlockSpec((D, D), const),
        pl.BlockSpec((2, D), lambda i: (0, 2)),
    ]
    args = [x, y1, y2, w, mod]
    if final_g is not None:
        in_specs.append(pl.BlockSpec((1, D), const))
        args.append(final_g)
    return pl.pallas_call(
        functools.partial(_out_kernel, mod_row=mod_row, final=final_g is not None),
        out_shape=jax.ShapeDtypeStruct((n_rows, D), F32),
        grid=(n_rows // TM,),
        in_specs=in_specs,
        out_specs=pl.BlockSpec((TM, D), lambda i: (i, 0)),
        compiler_params=_params(("parallel",)),
        name="out_proj",
    )(*args)


def _rope_tables():
    t = jnp.arange(S)
    row = (t // GRID_W).astype(F32)
    col = (t % GRID_W).astype(F32)
    quarter = HEAD // 4
    inv = ROPE_THETA ** (-jnp.arange(quarter, dtype=F32) / quarter)
    ang_r = row[:, None] * inv[None, :]
    ang_c = col[:, None] * inv[None, :]
    ang = jnp.concatenate([ang_r, ang_r, ang_c, ang_c], axis=-1)
    cos, sin = jnp.cos(ang), jnp.sin(ang)
    first = (jnp.arange(HEAD) % 32) < 16
    sa = jnp.where(first, -sin, 0.0)
    sb = jnp.where(first, 0.0, sin)
    pad = lambda a, v: jnp.tile(jnp.concatenate([a, jnp.full((L, HEAD), v, F32)], axis=0), (1, 2))
    return pad(cos, 1.0), pad(sa, 0.0), pad(sb, 0.0)


def kernel(x, c, ctx, c_ctx, ev_norm_g, ev_w_ada, ev_b_ada, ev_w_in, ev_a_sink, ev_b_lambda, ev_b_subln_g, ev_w_out, od_norm_g, od_w_ada, od_b_ada, od_w_in, od_c_q_norm_g, od_c_kv_norm_g, od_c_w_qb, od_c_w_kvb, od_d_rpb, od_w_out, final_norm_g):
    assert x.shape == (1, S, D) and ctx.shape == (1, L, D)
    assert ev_w_in.shape[0] == 1 and od_w_in.shape[0] == 1
    x0, xc0 = x[0], ctx[0]
    ct = jnp.stack([c[0], c_ctx], axis=1)
    tables = _rope_tables()
    qk_scale = HEAD ** -0.5

    mod = _ada(ct, ev_w_ada[0], ev_b_ada[0][None])
    w = ev_w_in[0]
    w_e = jnp.concatenate([w[:, :R_KA] * qk_scale, w[:, R_KA:R_QB], w[:, R_QB:R_KB] * qk_scale, w[:, R_KB:]],
                          axis=1).astype(BF16)
    proj, qt, vt = _proj_even(x0, xc0, mod, ev_norm_g[0][None], w_e, tables)
    ya = _mixer_a(qt, proj, vt, ev_a_sink[0])
    diff_extra = (ev_b_lambda[0], ev_b_subln_g[0][None])
    diff = functools.partial(_flash, qt, proj, vt, diff_extra, n_half=2, dq=128, qrow=EQ_B, kcol=E_KB, vrow=EV_B,
                             gcol=E_G + 512, n_heads=4)
    yb = diff(q_row0=0, n_q_rows=S, kv_row0=0, n_kv_rows=T, tq=FLASH_TQ, tk=FLASH_TK, unroll=FLASH_UNROLL,
              name="diff_attn")
    ybc = diff(q_row0=S, n_q_rows=L, kv_row0=S, n_kv_rows=L, tq=L, tk=L, unroll=1, name="diff_attn_ctx")
    w_out = ev_w_out[0].astype(BF16)
    x1 = _out_proj(x0, ya, 0, yb, w_out, mod, 0)
    xc1 = _out_proj(xc0, ya, S, ybc, w_out, mod, 1)

    mod = _ada(ct, od_w_ada[0], od_b_ada[0][None])
    w = od_w_in[0]
    w_o = jnp.concatenate([w[:, 0:384], w[:, 384:448], w[:, 384:448], w[:, 448:960] * qk_scale, w[:, 960:]],
                          axis=1).astype(BF16)
    wqb = od_c_w_qb[0].reshape(256, 4, MLA_DK)
    wqb = jnp.concatenate([wqb[:, :, :128].reshape(256, 512), wqb[:, :, 128:].reshape(256, 256)],
                          axis=1).astype(BF16)
    proj, qt, vt = _proj_odd(x1, xc1, mod, od_norm_g[0][None], w_o, od_c_q_norm_g[0][None],
                             od_c_kv_norm_g[0][None], wqb, od_c_w_kvb[0].astype(BF16), tables)
    ym = _flash(qt, proj, vt, (), n_half=1, dq=256, qrow=OQ_M, kcol=O_KM, vrow=OV_M, gcol=O_G, n_heads=4,
                q_row0=0, n_q_rows=S, kv_row0=0, n_kv_rows=T, tq=FLASH_TQ, tk=FLASH_TK, unroll=FLASH_UNROLL,
                name="mla_attn")
    yd = _mixer_d(qt, proj, vt, _neighbourhood_bias(od_d_rpb[0]))
    out = _out_proj(x1, ym, 0, yd, od_w_out[0].astype(BF16), mod, 0, final_norm_g[None])
    return out[None]
```

```python
import functools
import math

import jax
import jax.numpy as jnp
from jax import lax
from jax.experimental import pallas as pl
from jax.experimental.pallas import tpu as pltpu

F32 = jnp.float32
BF16 = jnp.bfloat16

D = 1024
S = 16384
L = 256
T = S + L
GRID_W = 64
EPS = 1e-6
NEG = -1e30
ROPE_THETA = 10000.0
A_WINDOW = 128
LANES = 128
HEAD = 64
MLA_DK = 192
LAM_INIT = 0.8 - 0.6 * math.exp(-0.3 * 0)

LOG2E = math.log2(math.e)
FLASH_TQ = 512
FLASH_TK = 1024
FLASH_UNROLL = 16
STALE_MAX_LIMIT = 30.0

TM = 256
OUT_TM = 512
N_LAT_TILES = S // TM
CTX_TILE = S // TM
VMEM_LIMIT = 56 * 1024 * 1024
KV_SLAB = 256

R_QA, R_KA, R_VA, R_QB, R_KB, R_VB, R_G = 0, 512, 640, 768, 1280, 1792, 2304
R_COLS = 3328
E_G, E_KB, E_KA = 0, 1024, 1536
E_COLS = 1664
EQ_A, EQ_B, EQ_ROWS = 0, 512, 1024
EV_B, EV_A, EV_ROWS = 0, 512, 640
O_G, O_KM, O_KD = 0, 1024, 2048
O_COLS = 2560
OQ_M, OQ_D, OQ_ROWS = 0, 1024, 1536
OV_M, OV_D, OV_ROWS = 0, 512, 1024


def _params(sem):
    return pltpu.CompilerParams(dimension_semantics=sem, vmem_limit_bytes=VMEM_LIMIT)


def _dot(a, b):
    return jnp.dot(a, b, preferred_element_type=F32)


def _silu(z):
    return z / (1.0 + jnp.exp(-z))


def _rms(z):
    return z * lax.rsqrt(jnp.mean(z * z, axis=-1, keepdims=True) + EPS)


def _ada_kernel(ct_ref, w_ref, b_ref, o_ref):
    sc = _silu(ct_ref[...])
    w = w_ref[...]
    r0 = jnp.sum(w * sc[:, 0:1], axis=0, keepdims=True)
    r1 = jnp.sum(w * sc[:, 1:2], axis=0, keepdims=True)
    o_ref[...] = jnp.concatenate([r0, r1], axis=0) + b_ref[...]


def _ada(ct, w, b):
    tn = 768
    return pl.pallas_call(
        _ada_kernel,
        out_shape=jax.ShapeDtypeStruct((2, 3 * D), F32),
        grid=(3 * D // tn,),
        in_specs=[
            pl.BlockSpec((D, 2), lambda n: (0, 0)),
            pl.BlockSpec((D, tn), lambda n: (0, n)),
            pl.BlockSpec((1, tn), lambda n: (0, n)),
        ],
        out_specs=pl.BlockSpec((2, tn), lambda n: (0, n)),
        compiler_params=_params(("parallel",)),
        name="ada_mod",
    )(ct, w, b)


def _is_latent_tile():
    return pl.program_id(0) < N_LAT_TILES


def _modulate(xin, mod_ref, ng_ref):
    mod = mod_ref[...]
    modr = jnp.where(_is_latent_tile(), mod[0:1], mod[1:2])
    shift, scale = modr[:, :D], modr[:, D:2 * D]
    return (_rms(xin) * ng_ref[...] * (1.0 + scale) + shift).astype(BF16)


def _modulated(x_ref, ctx_ref, mod_ref, ng_ref):
    return _modulate(jnp.where(_is_latent_tile(), x_ref[...], ctx_ref[...]), mod_ref, ng_ref)


def _rope_fn(cos_ref, sa_ref, sb_ref):
    cos, sa, sb = cos_ref[...], sa_ref[...], sb_ref[...]

    def rope(z):
        return z * cos + pltpu.roll(z, LANES - 16, 1) * sa + pltpu.roll(z, 16, 1) * sb

    return rope


def _tr(z):
    return z.T.astype(BF16)


def _slabs(lo, n):
    return [slice(lo + j * LANES, lo + (j + 1) * LANES) for j in range(n // LANES)]


def _proj_even_kernel(x_ref, ctx_ref, mod_ref, ng_ref, w_ref, cos_ref, sa_ref, sb_ref, o_ref, qt_ref, vt_ref):
    h = _modulated(x_ref, ctx_ref, mod_ref, ng_ref)
    res = _dot(h, w_ref[...])
    rope = _rope_fn(cos_ref, sa_ref, sb_ref)
    for src, dst in zip(_slabs(R_QA, 512) + _slabs(R_QB, 512), _slabs(EQ_A, 1024)):
        qt_ref[dst, :] = _tr(rope(res[:, src]) * LOG2E)
    for src, dst in zip(_slabs(R_KB, 512) + _slabs(R_KA, 128), _slabs(E_KB, 640)):
        o_ref[:, dst] = rope(res[:, src]).astype(BF16)
    for src, dst in zip(_slabs(R_VB, 512) + _slabs(R_VA, 128), _slabs(EV_B, 640)):
        vt_ref[0, dst, :] = _tr(res[:, src])
    o_ref[:, E_G:E_G + D] = _silu(res[:, R_G:]).astype(BF16)


def _proj_odd_kernel(x_ref, ctx_ref, ya_ref, yb_ref, ybc_ref, wout_ref, gate_ref, mod_ref, ng_ref, w_ref,
                     qg_ref, kvg_ref, wqb_ref, wkvb_ref, cos_ref, sa_ref, sb_ref, x1_ref, o_ref, qt_ref, vt_ref):
    is_lat = _is_latent_tile()
    y = jnp.concatenate([ya_ref[...], jnp.where(is_lat, yb_ref[...], ybc_ref[...])], axis=1)
    gate = jnp.where(is_lat, gate_ref[0:1, :], gate_ref[1:2, :])
    x1 = jnp.where(is_lat, x_ref[...], ctx_ref[...]) + gate * _dot(y, wout_ref[...])
    x1_ref[...] = x1
    h = _modulate(x1, mod_ref, ng_ref)
    res = _dot(h, w_ref[...])
    rope = _rope_fn(cos_ref, sa_ref, sb_ref)
    qn = (_rms(res[:, 0:256]) * qg_ref[...]).astype(BF16)
    kvn = (_rms(res[:, 256:384]) * kvg_ref[...]).astype(BF16)
    q = _dot(qn, wqb_ref[...]) * (MLA_DK ** -0.5 * LOG2E)
    kv = _dot(kvn, wkvb_ref[...])
    kpe = rope(res[:, 384:512]).astype(BF16)
    pes = (rope(q[:, 512:640]), rope(q[:, 640:768]))
    left = lax.broadcasted_iota(jnp.int32, (TM, LANES), 1) < HEAD
    for hd in range(4):
        pe = pes[hd // 2]
        pem = jnp.where(left if hd % 2 == 0 else jnp.logical_not(left), pe, 0.0)
        qt_ref[OQ_M + 256 * hd:OQ_M + 256 * hd + 128, :] = _tr(q[:, 128 * hd:128 * hd + 128])
        qt_ref[OQ_M + 256 * hd + 128:OQ_M + 256 * hd + 256, :] = _tr(pem)
        c0 = O_KM + 256 * hd
        o_ref[:, c0:c0 + 128] = kv[:, 256 * hd:256 * hd + 128].astype(BF16)
        o_ref[:, c0 + 128:c0 + 256] = kpe
        vt_ref[0, OV_M + 128 * hd:OV_M + 128 * hd + 128, :] = _tr(kv[:, 256 * hd + 128:256 * hd + 256])
    for src, dst in zip(_slabs(512, 512), _slabs(OQ_D, 512)):
        qt_ref[dst, :] = _tr(res[:, src] * LOG2E)
    o_ref[:, O_KD:O_KD + 512] = res[:, 1024:1536].astype(BF16)
    for src, dst in zip(_slabs(1536, 512), _slabs(OV_D, 512)):
        vt_ref[0, dst, :] = _tr(res[:, src])
    o_ref[:, O_G:O_G + D] = _silu(res[:, 2048:3072]).astype(BF16)


def _row_specs(n_w_cols):
    const = lambda i: (0, 0)
    return [
        pl.BlockSpec((TM, D), lambda i: (jnp.minimum(i, N_LAT_TILES - 1), 0)),
        pl.BlockSpec((L, D), const),
        pl.BlockSpec((2, 3 * D), const),
        pl.BlockSpec((1, D), const),
        pl.BlockSpec((D, n_w_cols), const),
    ]


def _table_specs():
    return [pl.BlockSpec((TM, LANES), lambda i: (i, 0))] * 3


def _proj_outs(n_cols, n_qt_rows, n_vt_rows):
    shapes = (jax.ShapeDtypeStruct((T, n_cols), BF16),
              jax.ShapeDtypeStruct((n_qt_rows, T), BF16),
              jax.ShapeDtypeStruct((T // KV_SLAB, n_vt_rows, KV_SLAB), BF16))
    specs = (pl.BlockSpec((TM, n_cols), lambda i: (i, 0)),
             pl.BlockSpec((n_qt_rows, TM), lambda i: (0, i)),
             pl.BlockSpec((1, n_vt_rows, KV_SLAB), lambda i: (i, 0, 0)))
    return shapes, specs


def _proj_even(x, ctx, mod, ng, w, tables):
    shapes, specs = _proj_outs(E_COLS, EQ_ROWS, EV_ROWS)
    return pl.pallas_call(
        _proj_even_kernel,
        out_shape=shapes,
        grid=(T // TM,),
        in_specs=_row_specs(R_COLS) + _table_specs(),
        out_specs=specs,
        compiler_params=_params(("parallel",)),
        name="proj_even",
    )(x, ctx, mod, ng, w, *tables)


def _proj_odd(x, ctx, ya, yb, ybc, w_out, mod_prev, mod, ng, w, qg, kvg, wqb, wkvb, tables):
    const = lambda i: (0, 0)
    lat = lambda i: (jnp.minimum(i, N_LAT_TILES - 1), 0)
    shapes, specs = _proj_outs(O_COLS, OQ_ROWS, OV_ROWS)
    row_specs = _row_specs(3072)
    return pl.pallas_call(
        _proj_odd_kernel,
        out_shape=(jax.ShapeDtypeStruct((T, D), F32),) + shapes,
        grid=(T // TM,),
        in_specs=row_specs[:2] + [
            pl.BlockSpec((TM, 512), lambda i: (i, 0)),
            pl.BlockSpec((TM, 512), lat),
            pl.BlockSpec((L, 512), const),
            pl.BlockSpec((D, D), const),
            pl.BlockSpec((2, D), lambda i: (0, 2)),
        ] + row_specs[2:] + [
            pl.BlockSpec((1, 256), const),
            pl.BlockSpec((1, 128), const),
            pl.BlockSpec((256, 768), const),
            pl.BlockSpec((128, 1024), const),
        ] + _table_specs(),
        out_specs=(pl.BlockSpec((TM, D), lambda i: (i, 0)),) + specs,
        compiler_params=_params(("parallel",)),
        name="proj_odd",
    )(x, ctx, ya, yb, ybc, w_out, mod_prev, mod, ng, w, qg, kvg, wqb, wkvb, *tables)


def _flash_kernel(*refs, n_half, n_chunks, tk, tail, unroll):
    if n_half == 2:
        qt_ref, k_ref, vt_ref, g_ref, bl_ref, sg_ref, o_ref = refs
    else:
        qt_ref, k_ref, vt_ref, g_ref, o_ref = refs
    qt = qt_ref[...]
    tq = qt.shape[1]
    if n_half == 2:
        z = jnp.zeros((HEAD, tq), BF16)
        qs = (jnp.concatenate([qt[:HEAD], z], axis=0), jnp.concatenate([z, qt[HEAD:]], axis=0))
    else:
        qs = (qt,)

    def update(r0, n, carry, exact_max):
        k = k_ref[pl.ds(r0, n), :]
        s0 = r0 // KV_SLAB
        vt = jnp.concatenate([vt_ref[s0 + j] for j in range(n // KV_SLAB)], axis=1)
        new = []
        for t in range(n_half):
            m_prev, l_prev, acc, over = carry[t]
            st = _dot(k, qs[t])
            m_new = jnp.maximum(m_prev, jnp.max(st, axis=0, keepdims=True))
            alpha = jnp.exp2(m_prev - m_new)
            if exact_max:
                pt = jnp.exp2(st - m_new)
                l_new = alpha * l_prev + jnp.sum(pt, axis=0, keepdims=True)
                acc = alpha * acc + _dot(vt, pt.astype(BF16))
            else:
                pt = jnp.exp2(st - m_prev)
                l_new = alpha * (l_prev + jnp.sum(pt, axis=0, keepdims=True))
                acc = alpha * (acc + _dot(vt, pt.astype(BF16)))
                over = jnp.maximum(over, m_new - m_prev)
            new.append((m_new, l_new, acc, over))
        return tuple(new)

    carry = tuple((jnp.full((1, tq), NEG, F32), jnp.zeros((1, tq), F32), jnp.zeros((LANES, tq), F32),
                   jnp.zeros((1, tq), F32)) for _ in range(n_half))
    if tail:
        carry = update(n_chunks * tk, tail, carry, True)
    if n_chunks == 1:
        carry = update(0, tk, carry, not tail)
    elif n_chunks > 1:
        def walk(exact_max, unroll_):
            return lambda cr: lax.fori_loop(
                0, n_chunks, lambda c, cr_: update(pl.multiple_of(c * tk, tk), tk, cr_, exact_max), cr,
                unroll=unroll_)

        seeded = carry
        carry = walk(False, unroll)(seeded)
        worst = jnp.max(jnp.concatenate([cr[3] for cr in carry], axis=1))
        carry = lax.cond(worst > STALE_MAX_LIMIT, lambda _: walk(True, 1)(seeded), lambda cr: cr, carry)
    if n_half == 2:
        bl = bl_ref[...]
        lam = (jnp.exp(jnp.sum(bl[0:1] * bl[1:2], axis=1, keepdims=True))
               - jnp.exp(jnp.sum(bl[2:3] * bl[3:4], axis=1, keepdims=True)) + LAM_INIT)
        ot = carry[0][2] / carry[0][1] - lam * (carry[1][2] / carry[1][1])
        o = _rms(ot.T) * sg_ref[...] * (1.0 - LAM_INIT)
    else:
        o = (carry[0][2] / carry[0][1]).T
    o_ref[...] = (o * g_ref[...].astype(F32)).astype(BF16)


def _flash(qt, proj, vt, extra, *, n_half, dq, qrow, kcol, vrow, gcol, n_heads, q_row0, n_q_rows, kv_row0,
           n_kv_rows, tq, tk, unroll, name):
    nq, n_chunks = n_q_rows // tq, n_kv_rows // tk
    tail = n_kv_rows - n_chunks * tk
    qb0, kb0 = q_row0 // tq, kv_row0 // n_kv_rows
    in_specs = [
        pl.BlockSpec((dq, tq), lambda h, i: (qrow // dq + h, qb0 + i)),
        pl.BlockSpec((n_kv_rows, dq), lambda h, i: (kb0, kcol // dq + h), pipeline_mode=pl.Buffered(1)),
        pl.BlockSpec((n_kv_rows // KV_SLAB, LANES, KV_SLAB), lambda h, i: (kb0, vrow // LANES + h, 0),
                     pipeline_mode=pl.Buffered(1)),
        pl.BlockSpec((tq, LANES), lambda h, i: (qb0 + i, gcol // LANES + h)),
    ]
    args = [qt, proj, vt, proj]
    if n_half == 2:
        in_specs += [pl.BlockSpec((4, HEAD), lambda h, i: (0, 0)),
                     pl.BlockSpec((1, LANES), lambda h, i: (0, 0))]
        args += list(extra)
    return pl.pallas_call(
        functools.partial(_flash_kernel, n_half=n_half, n_chunks=n_chunks, tk=tk, tail=tail, unroll=unroll),
        out_shape=jax.ShapeDtypeStruct((n_q_rows, n_heads * LANES), BF16),
        grid=(n_heads, nq),
        in_specs=in_specs,
        out_specs=pl.BlockSpec((tq, LANES), lambda h, i: (i, h)),
        compiler_params=_params(("parallel", "parallel")),
        name=name,
    )(*args)


def _zero_padded(qh, upper):
    z = jnp.zeros_like(qh)
    return jnp.concatenate([qh, z] if upper else [z, qh], axis=0)


def _store_head_pair(o_ref, g_ref, slab, top, bottom):
    o = jnp.concatenate([top, bottom], axis=0).T
    c = slice(slab * LANES, (slab + 1) * LANES)
    o_ref[:, c] = (o * g_ref[:, c].astype(F32)).astype(BF16)


def _local_softmax_pv(st_c, st_l, vt_c, vt_l, floor, exact_max):
    m_c = jnp.max(st_c, axis=0, keepdims=True)
    if floor is not None:
        m_c = jnp.maximum(m_c, floor)
    m_l = jnp.max(st_l, axis=0, keepdims=True)
    m = jnp.maximum(m_c, m_l) if exact_max else m_c
    pt_c = jnp.exp2(st_c - m)
    pt_l = jnp.exp2(st_l - m)
    denom = jnp.sum(pt_c, axis=0, keepdims=True) + jnp.sum(pt_l, axis=0, keepdims=True)
    if floor is not None:
        denom = denom + jnp.exp2(floor - m)
    ot = (_dot(vt_c, pt_c.astype(BF16)) + _dot(vt_l, pt_l.astype(BF16))) * (1.0 / denom)
    return ot, m_l - m_c


def _streamed_then_exact(run):
    worst = run(False)

    @pl.when(worst > STALE_MAX_LIMIT)
    def _():
        run(True)


def _mixer_a_kernel(sink_ref, qt_ref, k0_ref, k1_ref, k2_ref, k3_ref, kc_ref, v0_ref, v1_ref, v2_ref, vc_ref,
                    g_ref, o_ref):
    i = pl.program_id(0)
    n_loc = TM + 2 * A_WINDOW
    k_l = jnp.concatenate([k0_ref[...], k1_ref[...], k2_ref[...], k3_ref[...]], axis=0)
    k_c = kc_ref[...]
    vt_l = jnp.concatenate([v0_ref[0][:, TM - A_WINDOW:], v1_ref[0], v2_ref[0][:, :A_WINDOW]], axis=1)
    vt_c = vc_ref[0]
    kl = lax.broadcasted_iota(jnp.int32, (n_loc, TM), 0)
    ql = lax.broadcasted_iota(jnp.int32, (n_loc, TM), 1)
    kpos = i * TM - A_WINDOW + kl
    dist = kl - ql
    ok = (jnp.where(dist >= 0, 1, 0) * jnp.where(dist <= 2 * A_WINDOW, 1, 0) * jnp.where(kpos >= 0, 1, 0)
          * jnp.where(kpos < S, 1, 0) * jnp.where(i < N_LAT_TILES, 1, 0))
    mask = jnp.where(ok > 0, 0.0, NEG)
    mask4 = jnp.concatenate([mask] * 4, axis=1)
    qt = qt_ref[...]

    def run(exact_max):
        worst = None
        for kh in range(2):
            heads = range(4 * kh, 4 * kh + 4)
            w = jnp.concatenate([_zero_padded(qt[HEAD * hq:HEAD * (hq + 1)], kh == 0) for hq in heads], axis=1)
            snk = jnp.concatenate([jnp.full((1, TM), sink_ref[hq] * LOG2E, F32) for hq in heads], axis=1)
            ot, over = _local_softmax_pv(_dot(k_c, w), _dot(k_l, w) + mask4, vt_c, vt_l, snk, exact_max)
            ot = ot[HEAD * kh:HEAD * (kh + 1)]
            for pair in range(2):
                _store_head_pair(o_ref, g_ref, 2 * kh + pair, ot[:, 2 * pair * TM:(2 * pair + 1) * TM],
                                 ot[:, (2 * pair + 1) * TM:(2 * pair + 2) * TM])
            over = jnp.max(over)
            worst = over if worst is None else jnp.maximum(worst, over)
        return worst

    _streamed_then_exact(run)


def _mixer_a(qt, proj, vt, sink):
    nt = T // TM
    last_blk = T // A_WINDOW - 1

    def k_spec(off):
        return pl.BlockSpec((A_WINDOW, LANES),
                            lambda i: (jnp.clip(2 * i + off, 0, last_blk), E_KA // LANES))

    def v_spec(off):
        return pl.BlockSpec((1, LANES, KV_SLAB), lambda i: (jnp.clip(i + off, 0, nt - 1), EV_A // LANES, 0))

    return pl.pallas_call(
        _mixer_a_kernel,
        out_shape=jax.ShapeDtypeStruct((T, 512), BF16),
        grid=(nt,),
        in_specs=[pl.BlockSpec(memory_space=pltpu.SMEM),
                  pl.BlockSpec((512, TM), lambda i: (EQ_A // 512, i))]
        + [k_spec(off) for off in (-1, 0, 1, 2)]
        + [pl.BlockSpec((L, LANES), lambda i: (CTX_TILE, E_KA // LANES))]
        + [v_spec(off) for off in (-1, 0, 1)]
        + [pl.BlockSpec((1, LANES, KV_SLAB), lambda i: (CTX_TILE, EV_A // LANES, 0)),
           pl.BlockSpec((TM, 512), lambda i: (i, E_G // 512))],
        out_specs=pl.BlockSpec((TM, 512), lambda i: (i, 0)),
        compiler_params=_params(("parallel",)),
        name="mixer_a",
    )(sink, qt, *([proj] * 5), *([vt] * 4), proj)


def _mixer_d_kernel(qt_ref, k0_ref, k1_ref, k2_ref, kc_ref, v0_ref, v1_ref, v2_ref, vc_ref, bias_ref, g_ref, o_ref):
    k_l = jnp.concatenate([k0_ref[...], k1_ref[...], k2_ref[...]], axis=0)
    k_c = kc_ref[...]
    vt_l = jnp.concatenate([v0_ref[0], v1_ref[0], v2_ref[0]], axis=1)
    vt_c = vc_ref[0]
    qt = qt_ref[...]

    def run(exact_max):
        worst = None
        for pair in range(4):
            rows = slice(pair * LANES, (pair + 1) * LANES)
            w = jnp.concatenate([_zero_padded(qt[rows][:HEAD], True), _zero_padded(qt[rows][HEAD:], False)],
                                axis=1)
            bias = jnp.concatenate([bias_ref[0, 2 * pair], bias_ref[0, 2 * pair + 1]], axis=1)
            ot, over = _local_softmax_pv(_dot(k_c[:, rows], w), _dot(k_l[:, rows], w) + bias, vt_c[rows],
                                         vt_l[rows], None, exact_max)
            _store_head_pair(o_ref, g_ref, pair, ot[:HEAD, :TM], ot[HEAD:, TM:])
            over = jnp.max(over)
            worst = over if worst is None else jnp.maximum(worst, over)
        return worst

    _streamed_then_exact(run)


def _mixer_d(qt, proj, vt, bias):
    nt = S // TM

    def k_spec(off):
        return pl.BlockSpec((TM, 512), lambda i: (jnp.clip(i + off, 0, nt - 1), O_KD // 512))

    def v_spec(off):
        return pl.BlockSpec((1, 512, KV_SLAB), lambda i: (jnp.clip(i + off, 0, nt - 1), OV_D // 512, 0))

    def variant(i):
        return jnp.where(i == 0, 0, jnp.where(i == nt - 1, 2, 1))

    return pl.pallas_call(
        _mixer_d_kernel,
        out_shape=jax.ShapeDtypeStruct((S, 512), BF16),
        grid=(nt,),
        in_specs=[pl.BlockSpec((512, TM), lambda i: (OQ_D // 512, i))]
        + [k_spec(off) for off in (-1, 0, 1)]
        + [pl.BlockSpec((L, 512), lambda i: (CTX_TILE, O_KD // 512))]
        + [v_spec(off) for off in (-1, 0, 1)]
        + [pl.BlockSpec((1, 512, KV_SLAB), lambda i: (CTX_TILE, OV_D // 512, 0)),
           pl.BlockSpec((1, 8, 3 * TM, TM), lambda i: (variant(i), 0, 0, 0)),
           pl.BlockSpec((TM, 512), lambda i: (i, O_G // 512 + 1))],
        out_specs=pl.BlockSpec((TM, 512), lambda i: (i, 0)),
        compiler_params=_params(("parallel",)),
        name="mixer_d",
    )(qt, *([proj] * 4), *([vt] * 4), bias, proj)


D_ROWS, D_COLS = 8, 16
RPB_R, RPB_C = 2 * D_ROWS - 1, 2 * D_COLS - 1
Q_GRID_ROWS = TM // GRID_W
K_GRID_ROWS = 3 * Q_GRID_ROWS


def _bias_kernel(rpb_ref, o_ref):
    base = pl.program_id(0) * (RPB_R * RPB_C)
    kc = lax.broadcasted_iota(jnp.int32, (GRID_W, LANES), 0)
    lane = lax.broadcasted_iota(jnp.int32, (GRID_W, LANES), 1)
    right = lane >= GRID_W
    qc = jnp.where(right, lane - GRID_W, lane)
    cs = jnp.clip(qc - D_COLS // 2, 0, GRID_W - D_COLS)
    col_ok = jnp.where(kc >= cs, 1, 0) * jnp.where(kc < cs + D_COLS, 1, 0)
    dc = jnp.where(col_ok > 0, kc - qc + (D_COLS - 1), -1)
    dc_l = jnp.where(right, -1, dc)
    dc_r = jnp.where(right, dc, -1)
    neg = jnp.full((GRID_W, LANES), NEG, F32)
    pairs = []
    for d in range(RPB_R - 1):
        t2 = neg
        for j in range(RPB_C):
            t2 = jnp.where(dc_l == j, rpb_ref[base + (d + 1) * RPB_C + j] * LOG2E, t2)
            t2 = jnp.where(dc_r == j, rpb_ref[base + d * RPB_C + j] * LOG2E, t2)
        pairs.append(t2)
    row_ok = (lambda ri, kr: kr >= Q_GRID_ROWS,
              lambda ri, kr: 0 <= kr - ri < D_ROWS,
              lambda ri, kr: kr < D_ROWS)
    for var in range(3):
        for kr in range(K_GRID_ROWS):
            for b in range(Q_GRID_ROWS // 2):
                blk = pairs[kr - 2 * b + 2]
                ok_l, ok_r = row_ok[var](2 * b, kr), row_ok[var](2 * b + 1, kr)
                if not ok_l and not ok_r:
                    blk = neg
                elif not ok_l:
                    blk = jnp.where(right, blk, NEG)
                elif not ok_r:
                    blk = jnp.where(right, NEG, blk)
                o_ref[var, 0, kr * GRID_W:(kr + 1) * GRID_W, b * LANES:(b + 1) * LANES] = blk


def _neighbourhood_bias(rpb):
    n_heads = rpb.shape[0]
    return pl.pallas_call(
        _bias_kernel,
        out_shape=jax.ShapeDtypeStruct((3, n_heads, 3 * TM, TM), F32),
        grid=(n_heads,),
        in_specs=[pl.BlockSpec(memory_space=pltpu.SMEM)],
        out_specs=pl.BlockSpec((3, 1, 3 * TM, TM), lambda h: (0, h, 0, 0)),
        compiler_params=_params(("parallel",)),
        name="nbr_bias",
    )(rpb.reshape(-1))


def _final_kernel(x_ref, y1_ref, y2_ref, w_ref, gate_ref, fg_ref, o_ref):
    y = jnp.concatenate([y1_ref[...], y2_ref[...]], axis=1)
    xn = x_ref[...] + gate_ref[0:1, :] * _dot(y, w_ref[...])
    o_ref[...] = _rms(xn) * fg_ref[...]


def _final_proj(x, y1, y2, w, mod, final_g):
    const = lambda i: (0, 0)
    return pl.pallas_call(
        _final_kernel,
        out_shape=jax.ShapeDtypeStruct((S, D), F32),
        grid=(S // OUT_TM,),
        in_specs=[
            pl.BlockSpec((OUT_TM, D), lambda i: (i, 0)),
            pl.BlockSpec((OUT_TM, 512), lambda i: (i, 0)),
            pl.BlockSpec((OUT_TM, 512), lambda i: (i, 0)),
            pl.BlockSpec((D, D), const),
            pl.BlockSpec((2, D), lambda i: (0, 2)),
            pl.BlockSpec((1, D), const),
        ],
        out_specs=pl.BlockSpec((OUT_TM, D), lambda i: (i, 0)),
        compiler_params=_params(("parallel",)),
        name="final_proj",
    )(x, y1, y2, w, mod, final_g)


def _rope_tables():
    t = jnp.arange(S)
    row = (t // GRID_W).astype(F32)
    col = (t % GRID_W).astype(F32)
    quarter = HEAD // 4
    inv = ROPE_THETA ** (-jnp.arange(quarter, dtype=F32) / quarter)
    ang_r = row[:, None] * inv[None, :]
    ang_c = col[:, None] * inv[None, :]
    ang = jnp.concatenate([ang_r, ang_r, ang_c, ang_c], axis=-1)
    cos, sin = jnp.cos(ang), jnp.sin(ang)
    first = (jnp.arange(HEAD) % 32) < 16
    sa = jnp.where(first, -sin, 0.0)
    sb = jnp.where(first, 0.0, sin)
    pad = lambda a, v: jnp.tile(jnp.concatenate([a, jnp.full((L, HEAD), v, F32)], axis=0), (1, 2))
    return pad(cos, 1.0), pad(sa, 0.0), pad(sb, 0.0)


def kernel(x, c, ctx, c_ctx, ev_norm_g, ev_w_ada, ev_b_ada, ev_w_in, ev_a_sink, ev_b_lambda, ev_b_subln_g, ev_w_out, od_norm_g, od_w_ada, od_b_ada, od_w_in, od_c_q_norm_g, od_c_kv_norm_g, od_c_w_qb, od_c_w_kvb, od_d_rpb, od_w_out, final_norm_g):
    assert x.shape == (1, S, D) and ctx.shape == (1, L, D)
    assert ev_w_in.shape[0] == 1 and od_w_in.shape[0] == 1
    x0, xc0 = x[0], ctx[0]
    ct = jnp.stack([c[0], c_ctx], axis=1)
    tables = _rope_tables()
    qk_scale = HEAD ** -0.5

    mod = _ada(ct, ev_w_ada[0], ev_b_ada[0][None])
    w = ev_w_in[0]
    w_e = jnp.concatenate([w[:, :R_KA] * qk_scale, w[:, R_KA:R_QB], w[:, R_QB:R_KB] * qk_scale, w[:, R_KB:]],
                          axis=1).astype(BF16)
    proj, qt, vt = _proj_even(x0, xc0, mod, ev_norm_g[0][None], w_e, tables)
    ya = _mixer_a(qt, proj, vt, ev_a_sink[0])
    diff_extra = (ev_b_lambda[0], ev_b_subln_g[0][None])
    diff = functools.partial(_flash, qt, proj, vt, diff_extra, n_half=2, dq=128, qrow=EQ_B, kcol=E_KB, vrow=EV_B,
                             gcol=E_G + 512, n_heads=4)
    yb = diff(q_row0=0, n_q_rows=S, kv_row0=0, n_kv_rows=T, tq=FLASH_TQ, tk=FLASH_TK, unroll=FLASH_UNROLL,
              name="diff_attn")
    ybc = diff(q_row0=S, n_q_rows=L, kv_row0=S, n_kv_rows=L, tq=L, tk=L, unroll=1, name="diff_attn_ctx")
    mod_prev = mod

    mod = _ada(ct, od_w_ada[0], od_b_ada[0][None])
    w = od_w_in[0]
    w_o = jnp.concatenate([w[:, 0:384], w[:, 384:448], w[:, 384:448], w[:, 448:960] * qk_scale, w[:, 960:]],
                          axis=1).astype(BF16)
    wqb = od_c_w_qb[0].reshape(256, 4, MLA_DK)
    wqb = jnp.concatenate([wqb[:, :, :128].reshape(256, 512), wqb[:, :, 128:].reshape(256, 256)],
                          axis=1).astype(BF16)
    x1, proj, qt, vt = _proj_odd(x0, xc0, ya, yb, ybc, ev_w_out[0].astype(BF16), mod_prev, mod, od_norm_g[0][None],
                                 w_o, od_c_q_norm_g[0][None], od_c_kv_norm_g[0][None], wqb,
                                 od_c_w_kvb[0].astype(BF16), tables)
    ym = _flash(qt, proj, vt, (), n_half=1, dq=256, qrow=OQ_M, kcol=O_KM, vrow=OV_M, gcol=O_G, n_heads=4,
                q_row0=0, n_q_rows=S, kv_row0=0, n_kv_rows=T, tq=FLASH_TQ, tk=FLASH_TK, unroll=FLASH_UNROLL,
                name="mla_attn")
    yd = _mixer_d(qt, proj, vt, _neighbourhood_bias(od_d_rpb[0]))
    out = _final_proj(x1, ym, yd, od_w_out[0].astype(BF16), mod, final_norm_g[None])
    return out[None]
```

```python
import functools
import math

import jax
import jax.numpy as jnp
from jax import lax
from jax.experimental import pallas as pl
from jax.experimental.pallas import tpu as pltpu

F32 = jnp.float32
BF16 = jnp.bfloat16

D = 1024
S = 16384
L = 256
T = S + L
GRID_W = 64
EPS = 1e-6
NEG = -1e30
ROPE_THETA = 10000.0
A_WINDOW = 128
LANES = 128
HEAD = 64
MLA_DK = 192
LAM_INIT = 0.8 - 0.6 * math.exp(-0.3 * 0)

LOG2E = math.log2(math.e)
FLASH_TQ = 512
FLASH_TK = 1024
FLASH_UNROLL = 16
STALE_MAX_LIMIT = 30.0

TM = 256
Q_ROWS = TM // GRID_W
OUT_TM = 512
N_LAT_TILES = S // TM
CTX_TILE = S // TM
VMEM_LIMIT = 56 * 1024 * 1024
KV_SLAB = 256

R_QA, R_KA, R_VA, R_QB, R_KB, R_VB, R_G = 0, 512, 640, 768, 1280, 1792, 2304
R_COLS = 3328
E_G, E_KB, E_KA = 0, 1024, 1536
E_COLS = 1664
EQ_A, EQ_B, EQ_ROWS = 0, 512, 1024
EV_B, EV_A, EV_ROWS = 0, 512, 640
O_G, O_KM, O_KD = 0, 1024, 2048
O_COLS = 2560
OQ_M, OQ_D, OQ_ROWS = 0, 1024, 1536
OV_M, OV_D, OV_ROWS = 0, 512, 1024


def _params(sem):
    return pltpu.CompilerParams(dimension_semantics=sem, vmem_limit_bytes=VMEM_LIMIT)


def _dot(a, b):
    return jnp.dot(a, b, preferred_element_type=F32)


def _silu(z):
    return z / (1.0 + jnp.exp(-z))


def _rms(z):
    return z * lax.rsqrt(jnp.mean(z * z, axis=-1, keepdims=True) + EPS)


def _ada_kernel(ct_ref, w_ref, b_ref, o_ref):
    sc = _silu(ct_ref[...])
    w = w_ref[...]
    r0 = jnp.sum(w * sc[:, 0:1], axis=0, keepdims=True)
    r1 = jnp.sum(w * sc[:, 1:2], axis=0, keepdims=True)
    o_ref[...] = jnp.concatenate([r0, r1], axis=0) + b_ref[...]


def _ada(ct, w, b):
    tn = 768
    return pl.pallas_call(
        _ada_kernel,
        out_shape=jax.ShapeDtypeStruct((2, 3 * D), F32),
        grid=(3 * D // tn,),
        in_specs=[
            pl.BlockSpec((D, 2), lambda n: (0, 0)),
            pl.BlockSpec((D, tn), lambda n: (0, n)),
            pl.BlockSpec((1, tn), lambda n: (0, n)),
        ],
        out_specs=pl.BlockSpec((2, tn), lambda n: (0, n)),
        compiler_params=_params(("parallel",)),
        name="ada_mod",
    )(ct, w, b)


def _is_latent_tile():
    return pl.program_id(0) < N_LAT_TILES


def _modulate(xin, mod_ref, ng_ref):
    mod = mod_ref[...]
    modr = jnp.where(_is_latent_tile(), mod[0:1], mod[1:2])
    shift, scale = modr[:, :D], modr[:, D:2 * D]
    return (_rms(xin) * ng_ref[...] * (1.0 + scale) + shift).astype(BF16)


def _modulated(x_ref, ctx_ref, mod_ref, ng_ref):
    return _modulate(jnp.where(_is_latent_tile(), x_ref[...], ctx_ref[...]), mod_ref, ng_ref)


def _rope_fn(rt_ref, ct_ref):
    def table(k):
        rows = jnp.concatenate([jnp.broadcast_to(rt_ref[0, Q_ROWS * k + j:Q_ROWS * k + j + 1, :], (GRID_W, LANES))
                                for j in range(Q_ROWS)], axis=0)
        return rows + jnp.concatenate([ct_ref[0, k]] * Q_ROWS, axis=0)

    cos, sa, sb = table(0), table(1), table(2)

    def rope(z):
        return z * cos + pltpu.roll(z, LANES - 16, 1) * sa + pltpu.roll(z, 16, 1) * sb

    return rope


def _tr(z):
    return z.T.astype(BF16)


def _slabs(lo, n):
    return [slice(lo + j * LANES, lo + (j + 1) * LANES) for j in range(n // LANES)]


def _proj_even_kernel(x_ref, ctx_ref, mod_ref, ng_ref, w_ref, rt_ref, ct_ref, o_ref, qt_ref, vt_ref):
    h = _modulated(x_ref, ctx_ref, mod_ref, ng_ref)
    res = _dot(h, w_ref[...])
    rope = _rope_fn(rt_ref, ct_ref)
    for src, dst in zip(_slabs(R_QA, 512) + _slabs(R_QB, 512), _slabs(EQ_A, 1024)):
        qt_ref[dst, :] = _tr(rope(res[:, src]) * LOG2E)
    for src, dst in zip(_slabs(R_KB, 512) + _slabs(R_KA, 128), _slabs(E_KB, 640)):
        o_ref[:, dst] = rope(res[:, src]).astype(BF16)
    for src, dst in zip(_slabs(R_VB, 512) + _slabs(R_VA, 128), _slabs(EV_B, 640)):
        vt_ref[0, dst, :] = _tr(res[:, src])
    o_ref[:, E_G:E_G + D] = _silu(res[:, R_G:]).astype(BF16)


def _proj_odd_kernel(x_ref, ctx_ref, ya_ref, yb_ref, ybc_ref, wout_ref, gate_ref, mod_ref, ng_ref, w_ref,
                     qg_ref, kvg_ref, wqb_ref, wkvb_ref, rt_ref, ct_ref, x1_ref, o_ref, qt_ref, vt_ref):
    is_lat = _is_latent_tile()
    y = jnp.concatenate([ya_ref[...], jnp.where(is_lat, yb_ref[...], ybc_ref[...])], axis=1)
    gate = jnp.where(is_lat, gate_ref[0:1, :], gate_ref[1:2, :])
    x1 = jnp.where(is_lat, x_ref[...], ctx_ref[...]) + gate * _dot(y, wout_ref[...])
    x1_ref[...] = x1
    h = _modulate(x1, mod_ref, ng_ref)
    res = _dot(h, w_ref[...])
    rope = _rope_fn(rt_ref, ct_ref)
    qn = (_rms(res[:, 0:256]) * qg_ref[...]).astype(BF16)
    kvn = (_rms(res[:, 256:384]) * kvg_ref[...]).astype(BF16)
    q = _dot(qn, wqb_ref[...]) * (MLA_DK ** -0.5 * LOG2E)
    kv = _dot(kvn, wkvb_ref[...])
    kpe = rope(res[:, 384:512]).astype(BF16)
    pes = (rope(q[:, 512:640]), rope(q[:, 640:768]))
    left = lax.broadcasted_iota(jnp.int32, (TM, LANES), 1) < HEAD
    for hd in range(4):
        pe = pes[hd // 2]
        pem = jnp.where(left if hd % 2 == 0 else jnp.logical_not(left), pe, 0.0)
        qt_ref[OQ_M + 256 * hd:OQ_M + 256 * hd + 128, :] = _tr(q[:, 128 * hd:128 * hd + 128])
        qt_ref[OQ_M + 256 * hd + 128:OQ_M + 256 * hd + 256, :] = _tr(pem)
        c0 = O_KM + 256 * hd
        o_ref[:, c0:c0 + 128] = kv[:, 256 * hd:256 * hd + 128].astype(BF16)
        o_ref[:, c0 + 128:c0 + 256] = kpe
        vt_ref[0, OV_M + 128 * hd:OV_M + 128 * hd + 128, :] = _tr(kv[:, 256 * hd + 128:256 * hd + 256])
    for src, dst in zip(_slabs(512, 512), _slabs(OQ_D, 512)):
        qt_ref[dst, :] = _tr(res[:, src] * LOG2E)
    o_ref[:, O_KD:O_KD + 512] = res[:, 1024:1536].astype(BF16)
    for src, dst in zip(_slabs(1536, 512), _slabs(OV_D, 512)):
        vt_ref[0, dst, :] = _tr(res[:, src])
    o_ref[:, O_G:O_G + D] = _silu(res[:, 2048:3072]).astype(BF16)


def _row_specs(n_w_cols):
    const = lambda i: (0, 0)
    return [
        pl.BlockSpec((TM, D), lambda i: (jnp.minimum(i, N_LAT_TILES - 1), 0)),
        pl.BlockSpec((L, D), const),
        pl.BlockSpec((2, 3 * D), const),
        pl.BlockSpec((1, D), const),
        pl.BlockSpec((D, n_w_cols), const),
    ]


def _table_specs():
    return [pl.BlockSpec((1, 3 * Q_ROWS, LANES), lambda i: (i, 0, 0)),
            pl.BlockSpec((1, 3, GRID_W, LANES), lambda i: (jnp.where(i < N_LAT_TILES, 0, 1), 0, 0, 0))]


def _proj_outs(n_cols, n_qt_rows, n_vt_rows):
    shapes = (jax.ShapeDtypeStruct((T, n_cols), BF16),
              jax.ShapeDtypeStruct((n_qt_rows, T), BF16),
              jax.ShapeDtypeStruct((T // KV_SLAB, n_vt_rows, KV_SLAB), BF16))
    specs = (pl.BlockSpec((TM, n_cols), lambda i: (i, 0)),
             pl.BlockSpec((n_qt_rows, TM), lambda i: (0, i)),
             pl.BlockSpec((1, n_vt_rows, KV_SLAB), lambda i: (i, 0, 0)))
    return shapes, specs


def _proj_even(x, ctx, mod, ng, w, tables):
    shapes, specs = _proj_outs(E_COLS, EQ_ROWS, EV_ROWS)
    return pl.pallas_call(
        _proj_even_kernel,
        out_shape=shapes,
        grid=(T // TM,),
        in_specs=_row_specs(R_COLS) + _table_specs(),
        out_specs=specs,
        compiler_params=_params(("parallel",)),
        name="proj_even",
    )(x, ctx, mod, ng, w, *tables)


def _proj_odd(x, ctx, ya, yb, ybc, w_out, mod_prev, mod, ng, w, qg, kvg, wqb, wkvb, tables):
    const = lambda i: (0, 0)
    lat = lambda i: (jnp.minimum(i, N_LAT_TILES - 1), 0)
    shapes, specs = _proj_outs(O_COLS, OQ_ROWS, OV_ROWS)
    row_specs = _row_specs(3072)
    return pl.pallas_call(
        _proj_odd_kernel,
        out_shape=(jax.ShapeDtypeStruct((T, D), F32),) + shapes,
        grid=(T // TM,),
        in_specs=row_specs[:2] + [
            pl.BlockSpec((TM, 512), lambda i: (i, 0)),
            pl.BlockSpec((TM, 512), lat),
            pl.BlockSpec((L, 512), const),
            pl.BlockSpec((D, D), const),
            pl.BlockSpec((2, D), lambda i: (0, 2)),
        ] + row_specs[2:] + [
            pl.BlockSpec((1, 256), const),
            pl.BlockSpec((1, 128), const),
            pl.BlockSpec((256, 768), const),
            pl.BlockSpec((128, 1024), const),
        ] + _table_specs(),
        out_specs=(pl.BlockSpec((TM, D), lambda i: (i, 0)),) + specs,
        compiler_params=_params(("parallel",)),
        name="proj_odd",
    )(x, ctx, ya, yb, ybc, w_out, mod_prev, mod, ng, w, qg, kvg, wqb, wkvb, *tables)


def _flash_kernel(*refs, n_half, n_chunks, tk, tail, unroll):
    if n_half == 2:
        qt_ref, k_ref, vt_ref, g_ref, bl_ref, sg_ref, o_ref = refs
    else:
        qt_ref, k_ref, vt_ref, g_ref, o_ref = refs
    qt = qt_ref[...]
    tq = qt.shape[1]
    if n_half == 2:
        z = jnp.zeros((HEAD, tq), BF16)
        qs = (jnp.concatenate([qt[:HEAD], z], axis=0), jnp.concatenate([z, qt[HEAD:]], axis=0))
    else:
        qs = (qt,)

    def update(r0, n, carry, exact_max):
        k = k_ref[pl.ds(r0, n), :]
        s0 = r0 // KV_SLAB
        vt = jnp.concatenate([vt_ref[s0 + j] for j in range(n // KV_SLAB)], axis=1)
        new = []
        for t in range(n_half):
            m_prev, l_prev, acc, over = carry[t]
            st = _dot(k, qs[t])
            m_new = jnp.maximum(m_prev, jnp.max(st, axis=0, keepdims=True))
            alpha = jnp.exp2(m_prev - m_new)
            if exact_max:
                pt = jnp.exp2(st - m_new)
                l_new = alpha * l_prev + jnp.sum(pt, axis=0, keepdims=True)
                acc = alpha * acc + _dot(vt, pt.astype(BF16))
            else:
                pt = jnp.exp2(st - m_prev)
                l_new = alpha * (l_prev + jnp.sum(pt, axis=0, keepdims=True))
                acc = alpha * (acc + _dot(vt, pt.astype(BF16)))
                over = jnp.maximum(over, m_new - m_prev)
            new.append((m_new, l_new, acc, over))
        return tuple(new)

    carry = tuple((jnp.full((1, tq), NEG, F32), jnp.zeros((1, tq), F32), jnp.zeros((LANES, tq), F32),
                   jnp.zeros((1, tq), F32)) for _ in range(n_half))
    if tail:
        carry = update(n_chunks * tk, tail, carry, True)
    if n_chunks == 1:
        carry = update(0, tk, carry, not tail)
    elif n_chunks > 1:
        def walk(exact_max, unroll_):
            return lambda cr: lax.fori_loop(
                0, n_chunks, lambda c, cr_: update(pl.multiple_of(c * tk, tk), tk, cr_, exact_max), cr,
                unroll=unroll_)

        seeded = carry
        carry = walk(False, unroll)(seeded)
        worst = jnp.max(jnp.concatenate([cr[3] for cr in carry], axis=1))
        carry = lax.cond(worst > STALE_MAX_LIMIT, lambda _: walk(True, 1)(seeded), lambda cr: cr, carry)
    if n_half == 2:
        bl = bl_ref[...]
        lam = (jnp.exp(jnp.sum(bl[0:1] * bl[1:2], axis=1, keepdims=True))
               - jnp.exp(jnp.sum(bl[2:3] * bl[3:4], axis=1, keepdims=True)) + LAM_INIT)
        ot = carry[0][2] / carry[0][1] - lam * (carry[1][2] / carry[1][1])
        o = _rms(ot.T) * sg_ref[...] * (1.0 - LAM_INIT)
    else:
        o = (carry[0][2] / carry[0][1]).T
    o_ref[...] = (o * g_ref[...].astype(F32)).astype(BF16)


def _flash(qt, proj, vt, extra, *, n_half, dq, qrow, kcol, vrow, gcol, n_heads, q_row0, n_q_rows, kv_row0,
           n_kv_rows, tq, tk, unroll, name):
    nq, n_chunks = n_q_rows // tq, n_kv_rows // tk
    tail = n_kv_rows - n_chunks * tk
    qb0, kb0 = q_row0 // tq, kv_row0 // n_kv_rows
    in_specs = [
        pl.BlockSpec((dq, tq), lambda h, i: (qrow // dq + h, qb0 + i)),
        pl.BlockSpec((n_kv_rows, dq), lambda h, i: (kb0, kcol // dq + h), pipeline_mode=pl.Buffered(1)),
        pl.BlockSpec((n_kv_rows // KV_SLAB, LANES, KV_SLAB), lambda h, i: (kb0, vrow // LANES + h, 0),
                     pipeline_mode=pl.Buffered(1)),
        pl.BlockSpec((tq, LANES), lambda h, i: (qb0 + i, gcol // LANES + h)),
    ]
    args = [qt, proj, vt, proj]
    if n_half == 2:
        in_specs += [pl.BlockSpec((4, HEAD), lambda h, i: (0, 0)),
                     pl.BlockSpec((1, LANES), lambda h, i: (0, 0))]
        args += list(extra)
    return pl.pallas_call(
        functools.partial(_flash_kernel, n_half=n_half, n_chunks=n_chunks, tk=tk, tail=tail, unroll=unroll),
        out_shape=jax.ShapeDtypeStruct((n_q_rows, n_heads * LANES), BF16),
        grid=(n_heads, nq),
        in_specs=in_specs,
        out_specs=pl.BlockSpec((tq, LANES), lambda h, i: (i, h)),
        compiler_params=_params(("parallel", "parallel")),
        name=name,
    )(*args)


def _zero_padded(qh, upper):
    z = jnp.zeros_like(qh)
    return jnp.concatenate([qh, z] if upper else [z, qh], axis=0)


def _store_head_pair(o_ref, g_ref, slab, top, bottom):
    o = jnp.concatenate([top, bottom], axis=0).T
    c = slice(slab * LANES, (slab + 1) * LANES)
    o_ref[:, c] = (o * g_ref[:, c].astype(F32)).astype(BF16)


def _local_softmax_pv(st_c, st_l, vt_c, vt_l, floor, exact_max):
    m_c = jnp.max(st_c, axis=0, keepdims=True)
    if floor is not None:
        m_c = jnp.maximum(m_c, floor)
    m_l = jnp.max(st_l, axis=0, keepdims=True)
    m = jnp.maximum(m_c, m_l) if exact_max else m_c
    pt_c = jnp.exp2(st_c - m)
    pt_l = jnp.exp2(st_l - m)
    denom = jnp.sum(pt_c, axis=0, keepdims=True) + jnp.sum(pt_l, axis=0, keepdims=True)
    if floor is not None:
        denom = denom + jnp.exp2(floor - m)
    ot = (_dot(vt_c, pt_c.astype(BF16)) + _dot(vt_l, pt_l.astype(BF16))) * (1.0 / denom)
    return ot, m_l - m_c


def _streamed_then_exact(run):
    worst = run(False)

    @pl.when(worst > STALE_MAX_LIMIT)
    def _():
        run(True)


def _mixer_a_kernel(sink_ref, qt_ref, k0_ref, k1_ref, k2_ref, k3_ref, kc_ref, v0_ref, v1_ref, v2_ref, vc_ref,
                    g_ref, o_ref):
    i = pl.program_id(0)
    n_loc = TM + 2 * A_WINDOW
    k_l = jnp.concatenate([k0_ref[...], k1_ref[...], k2_ref[...], k3_ref[...]], axis=0)
    k_c = kc_ref[...]
    vt_l = jnp.concatenate([v0_ref[0][:, TM - A_WINDOW:], v1_ref[0], v2_ref[0][:, :A_WINDOW]], axis=1)
    vt_c = vc_ref[0]
    kl = lax.broadcasted_iota(jnp.int32, (n_loc, TM), 0)
    ql = lax.broadcasted_iota(jnp.int32, (n_loc, TM), 1)
    kpos = i * TM - A_WINDOW + kl
    dist = kl - ql
    ok = (jnp.where(dist >= 0, 1, 0) * jnp.where(dist <= 2 * A_WINDOW, 1, 0) * jnp.where(kpos >= 0, 1, 0)
          * jnp.where(kpos < S, 1, 0) * jnp.where(i < N_LAT_TILES, 1, 0))
    mask = jnp.where(ok > 0, 0.0, NEG)
    mask4 = jnp.concatenate([mask] * 4, axis=1)
    qt = qt_ref[...]

    def run(exact_max):
        worst = None
        for kh in range(2):
            heads = range(4 * kh, 4 * kh + 4)
            w = jnp.concatenate([_zero_padded(qt[HEAD * hq:HEAD * (hq + 1)], kh == 0) for hq in heads], axis=1)
            snk = jnp.concatenate([jnp.full((1, TM), sink_ref[hq] * LOG2E, F32) for hq in heads], axis=1)
            ot, over = _local_softmax_pv(_dot(k_c, w), _dot(k_l, w) + mask4, vt_c, vt_l, snk, exact_max)
            ot = ot[HEAD * kh:HEAD * (kh + 1)]
            for pair in range(2):
                _store_head_pair(o_ref, g_ref, 2 * kh + pair, ot[:, 2 * pair * TM:(2 * pair + 1) * TM],
                                 ot[:, (2 * pair + 1) * TM:(2 * pair + 2) * TM])
            over = jnp.max(over)
            worst = over if worst is None else jnp.maximum(worst, over)
        return worst

    _streamed_then_exact(run)


def _mixer_a(qt, proj, vt, sink):
    nt = T // TM
    last_blk = T // A_WINDOW - 1

    def k_spec(off):
        return pl.BlockSpec((A_WINDOW, LANES),
                            lambda i: (jnp.clip(2 * i + off, 0, last_blk), E_KA // LANES))

    def v_spec(off):
        return pl.BlockSpec((1, LANES, KV_SLAB), lambda i: (jnp.clip(i + off, 0, nt - 1), EV_A // LANES, 0))

    return pl.pallas_call(
        _mixer_a_kernel,
        out_shape=jax.ShapeDtypeStruct((T, 512), BF16),
        grid=(nt,),
        in_specs=[pl.BlockSpec(memory_space=pltpu.SMEM),
                  pl.BlockSpec((512, TM), lambda i: (EQ_A // 512, i))]
        + [k_spec(off) for off in (-1, 0, 1, 2)]
        + [pl.BlockSpec((L, LANES), lambda i: (CTX_TILE, E_KA // LANES))]
        + [v_spec(off) for off in (-1, 0, 1)]
        + [pl.BlockSpec((1, LANES, KV_SLAB), lambda i: (CTX_TILE, EV_A // LANES, 0)),
           pl.BlockSpec((TM, 512), lambda i: (i, E_G // 512))],
        out_specs=pl.BlockSpec((TM, 512), lambda i: (i, 0)),
        compiler_params=_params(("parallel",)),
        name="mixer_a",
    )(sink, qt, *([proj] * 5), *([vt] * 4), proj)


def _mixer_d_kernel(qt_ref, k0_ref, k1_ref, k2_ref, kc_ref, v0_ref, v1_ref, v2_ref, vc_ref, bias_ref, g_ref, o_ref):
    k_l = jnp.concatenate([k0_ref[...], k1_ref[...], k2_ref[...]], axis=0)
    k_c = kc_ref[...]
    vt_l = jnp.concatenate([v0_ref[0], v1_ref[0], v2_ref[0]], axis=1)
    vt_c = vc_ref[0]
    qt = qt_ref[...]

    def run(exact_max):
        worst = None
        for pair in range(4):
            rows = slice(pair * LANES, (pair + 1) * LANES)
            w = jnp.concatenate([_zero_padded(qt[rows][:HEAD], True), _zero_padded(qt[rows][HEAD:], False)],
                                axis=1)
            bias = jnp.concatenate([bias_ref[0, 2 * pair], bias_ref[0, 2 * pair + 1]], axis=1)
            ot, over = _local_softmax_pv(_dot(k_c[:, rows], w), _dot(k_l[:, rows], w) + bias, vt_c[rows],
                                         vt_l[rows], None, exact_max)
            _store_head_pair(o_ref, g_ref, pair, ot[:HEAD, :TM], ot[HEAD:, TM:])
            over = jnp.max(over)
            worst = over if worst is None else jnp.maximum(worst, over)
        return worst

    _streamed_then_exact(run)


def _mixer_d(qt, proj, vt, bias):
    nt = S // TM

    def k_spec(off):
        return pl.BlockSpec((TM, 512), lambda i: (jnp.clip(i + off, 0, nt - 1), O_KD // 512))

    def v_spec(off):
        return pl.BlockSpec((1, 512, KV_SLAB), lambda i: (jnp.clip(i + off, 0, nt - 1), OV_D // 512, 0))

    def variant(i):
        return jnp.where(i == 0, 0, jnp.where(i == nt - 1, 2, 1))

    return pl.pallas_call(
        _mixer_d_kernel,
        out_shape=jax.ShapeDtypeStruct((S, 512), BF16),
        grid=(nt,),
        in_specs=[pl.BlockSpec((512, TM), lambda i: (OQ_D // 512, i))]
        + [k_spec(off) for off in (-1, 0, 1)]
        + [pl.BlockSpec((L, 512), lambda i: (CTX_TILE, O_KD // 512))]
        + [v_spec(off) for off in (-1, 0, 1)]
        + [pl.BlockSpec((1, 512, KV_SLAB), lambda i: (CTX_TILE, OV_D // 512, 0)),
           pl.BlockSpec((1, 8, 3 * TM, TM), lambda i: (variant(i), 0, 0, 0)),
           pl.BlockSpec((TM, 512), lambda i: (i, O_G // 512 + 1))],
        out_specs=pl.BlockSpec((TM, 512), lambda i: (i, 0)),
        compiler_params=_params(("parallel",)),
        name="mixer_d",
    )(qt, *([proj] * 4), *([vt] * 4), bias, proj)


D_ROWS, D_COLS = 8, 16
RPB_R, RPB_C = 2 * D_ROWS - 1, 2 * D_COLS - 1
Q_GRID_ROWS = Q_ROWS
K_GRID_ROWS = 3 * Q_GRID_ROWS


def _bias_kernel(rpb_ref, o_ref):
    base = pl.program_id(0) * (RPB_R * RPB_C)
    kc = lax.broadcasted_iota(jnp.int32, (GRID_W, LANES), 0)
    lane = lax.broadcasted_iota(jnp.int32, (GRID_W, LANES), 1)
    right = lane >= GRID_W
    qc = jnp.where(right, lane - GRID_W, lane)
    cs = jnp.clip(qc - D_COLS // 2, 0, GRID_W - D_COLS)
    col_ok = jnp.where(kc >= cs, 1, 0) * jnp.where(kc < cs + D_COLS, 1, 0)
    dc = jnp.where(col_ok > 0, kc - qc + (D_COLS - 1), -1)
    dc_l = jnp.where(right, -1, dc)
    dc_r = jnp.where(right, dc, -1)
    neg = jnp.full((GRID_W, LANES), NEG, F32)
    pairs = []
    for d in range(RPB_R - 1):
        t2 = neg
        for j in range(RPB_C):
            t2 = jnp.where(dc_l == j, rpb_ref[base + (d + 1) * RPB_C + j] * LOG2E, t2)
            t2 = jnp.where(dc_r == j, rpb_ref[base + d * RPB_C + j] * LOG2E, t2)
        pairs.append(t2)
    row_ok = (lambda ri, kr: kr >= Q_GRID_ROWS,
              lambda ri, kr: 0 <= kr - ri < D_ROWS,
              lambda ri, kr: kr < D_ROWS)
    for var in range(3):
        for kr in range(K_GRID_ROWS):
            for b in range(Q_GRID_ROWS // 2):
                blk = pairs[kr - 2 * b + 2]
                ok_l, ok_r = row_ok[var](2 * b, kr), row_ok[var](2 * b + 1, kr)
                if not ok_l and not ok_r:
                    blk = neg
                elif not ok_l:
                    blk = jnp.where(right, blk, NEG)
                elif not ok_r:
                    blk = jnp.where(right, NEG, blk)
                o_ref[var, 0, kr * GRID_W:(kr + 1) * GRID_W, b * LANES:(b + 1) * LANES] = blk


def _neighbourhood_bias(rpb):
    n_heads = rpb.shape[0]
    return pl.pallas_call(
        _bias_kernel,
        out_shape=jax.ShapeDtypeStruct((3, n_heads, 3 * TM, TM), F32),
        grid=(n_heads,),
        in_specs=[pl.BlockSpec(memory_space=pltpu.SMEM)],
        out_specs=pl.BlockSpec((3, 1, 3 * TM, TM), lambda h: (0, h, 0, 0)),
        compiler_params=_params(("parallel",)),
        name="nbr_bias",
    )(rpb.reshape(-1))


def _final_kernel(x_ref, y1_ref, y2_ref, w_ref, gate_ref, fg_ref, o_ref):
    y = jnp.concatenate([y1_ref[...], y2_ref[...]], axis=1)
    xn = x_ref[...] + gate_ref[0:1, :] * _dot(y, w_ref[...])
    o_ref[...] = _rms(xn) * fg_ref[...]


def _final_proj(x, y1, y2, w, mod, final_g):
    const = lambda i: (0, 0)
    return pl.pallas_call(
        _final_kernel,
        out_shape=jax.ShapeDtypeStruct((S, D), F32),
        grid=(S // OUT_TM,),
        in_specs=[
            pl.BlockSpec((OUT_TM, D), lambda i: (i, 0)),
            pl.BlockSpec((OUT_TM, 512), lambda i: (i, 0)),
            pl.BlockSpec((OUT_TM, 512), lambda i: (i, 0)),
            pl.BlockSpec((D, D), const),
            pl.BlockSpec((2, D), lambda i: (0, 2)),
            pl.BlockSpec((1, D), const),
        ],
        out_specs=pl.BlockSpec((OUT_TM, D), lambda i: (i, 0)),
        compiler_params=_params(("parallel",)),
        name="final_proj",
    )(x, y1, y2, w, mod, final_g)


def _rope_tables():
    quarter = HEAD // 4
    inv = ROPE_THETA ** (-jnp.arange(quarter, dtype=F32) / quarter)

    def parts(n):
        ang = jnp.arange(n).astype(F32)[:, None] * inv[None, :]
        cos, sin, z = jnp.cos(ang), jnp.sin(ang), jnp.zeros((n, quarter), F32)
        return (jnp.concatenate([cos, cos], axis=1), jnp.concatenate([-sin, z], axis=1),
                jnp.concatenate([z, sin], axis=1))

    def lanes(first32, second32):
        return jnp.tile(jnp.concatenate([first32, second32], axis=1), (1, LANES // HEAD))

    n_rows = S // GRID_W
    zr, zc = jnp.zeros((n_rows, 32), F32), jnp.zeros((GRID_W, 32), F32)
    ident = (jnp.ones((Q_ROWS, 32), F32), jnp.zeros((Q_ROWS, 32), F32), jnp.zeros((Q_ROWS, 32), F32))
    rt = jnp.concatenate([
        jnp.concatenate([lanes(p, zr).reshape(N_LAT_TILES, Q_ROWS, LANES) for p in parts(n_rows)], axis=1),
        jnp.concatenate([lanes(p, jnp.zeros((Q_ROWS, 32), F32)) for p in ident], axis=0)[None]], axis=0)
    ct_lat = jnp.stack([lanes(zc, p) for p in parts(GRID_W)])
    ct_ctx = jnp.stack([lanes(zc, jnp.full((GRID_W, 32), v, F32)) for v in (1.0, 0.0, 0.0)])
    return rt, jnp.stack([ct_lat, ct_ctx])


def kernel(x, c, ctx, c_ctx, ev_norm_g, ev_w_ada, ev_b_ada, ev_w_in, ev_a_sink, ev_b_lambda, ev_b_subln_g, ev_w_out, od_norm_g, od_w_ada, od_b_ada, od_w_in, od_c_q_norm_g, od_c_kv_norm_g, od_c_w_qb, od_c_w_kvb, od_d_rpb, od_w_out, final_norm_g):
    assert x.shape == (1, S, D) and ctx.shape == (1, L, D)
    assert ev_w_in.shape[0] == 1 and od_w_in.shape[0] == 1
    x0, xc0 = x[0], ctx[0]
    ct = jnp.stack([c[0], c_ctx], axis=1)
    tables = _rope_tables()
    qk_scale = HEAD ** -0.5

    mod = _ada(ct, ev_w_ada[0], ev_b_ada[0][None])
    w = ev_w_in[0]
    w_e = jnp.concatenate([w[:, :R_KA] * qk_scale, w[:, R_KA:R_QB], w[:, R_QB:R_KB] * qk_scale, w[:, R_KB:]],
                          axis=1).astype(BF16)
    proj, qt, vt = _proj_even(x0, xc0, mod, ev_norm_g[0][None], w_e, tables)
    ya = _mixer_a(qt, proj, vt, ev_a_sink[0])
    diff_extra = (ev_b_lambda[0], ev_b_subln_g[0][None])
    diff = functools.partial(_flash, qt, proj, vt, diff_extra, n_half=2, dq=128, qrow=EQ_B, kcol=E_KB, vrow=EV_B,
                             gcol=E_G + 512, n_heads=4)
    yb = diff(q_row0=0, n_q_rows=S, kv_row0=0, n_kv_rows=T, tq=FLASH_TQ, tk=FLASH_TK, unroll=FLASH_UNROLL,
              name="diff_attn")
    ybc = diff(q_row0=S, n_q_rows=L, kv_row0=S, n_kv_rows=L, tq=L, tk=L, unroll=1, name="diff_attn_ctx")
    mod_prev = mod

    mod = _ada(ct, od_w_ada[0], od_b_ada[0][None])
    w = od_w_in[0]
    w_o = jnp.concatenate([w[:, 0:384], w[:, 384:448], w[:, 384:448], w[:, 448:960] * qk_scale, w[:, 960:]],
                          axis=1).astype(BF16)
    wqb = od_c_w_qb[0].reshape(256, 4, MLA_DK)
    wqb = jnp.concatenate([wqb[:, :, :128].reshape(256, 512), wqb[:, :, 128:].reshape(256, 256)],
                          axis=1).astype(BF16)
    x1, proj, qt, vt = _proj_odd(x0, xc0, ya, yb, ybc, ev_w_out[0].astype(BF16), mod_prev, mod, od_norm_g[0][None],
                                 w_o, od_c_q_norm_g[0][None], od_c_kv_norm_g[0][None], wqb,
                                 od_c_w_kvb[0].astype(BF16), tables)
    ym = _flash(qt, proj, vt, (), n_half=1, dq=256, qrow=OQ_M, kcol=O_KM, vrow=OV_M, gcol=O_G, n_heads=4,
                q_row0=0, n_q_rows=S, kv_row0=0, n_kv_rows=T, tq=FLASH_TQ, tk=FLASH_TK, unroll=FLASH_UNROLL,
                name="mla_attn")
    yd = _mixer_d(qt, proj, vt, _neighbourhood_bias(od_d_rpb[0]))
    out = _final_proj(x1, ym, yd, od_w_out[0].astype(BF16), mod, final_norm_g[None])
    return out[None]
```

```python
import functools
import math

import jax
import jax.numpy as jnp
from jax import lax
from jax.experimental import pallas as pl
from jax.experimental.pallas import tpu as pltpu

F32 = jnp.float32
BF16 = jnp.bfloat16

D = 1024
S = 16384
L = 256
T = S + L
GRID_W = 64
EPS = 1e-6
NEG = -1e30
ROPE_THETA = 10000.0
A_WINDOW = 128
LANES = 128
HEAD = 64
MLA_DK = 192
LAM_INIT = 0.8 - 0.6 * math.exp(-0.3 * 0)

LOG2E = math.log2(math.e)
FLASH_TQ = 512
FLASH_TK = 1024
FLASH_UNROLL = 16

TM = 256
Q_ROWS = TM // GRID_W
OUT_TM = 512
N_LAT_TILES = S // TM
CTX_TILE = S // TM
VMEM_LIMIT = 56 * 1024 * 1024
KV_SLAB = 256

R_QA, R_KA, R_VA, R_QB, R_KB, R_VB, R_G = 0, 512, 640, 768, 1280, 1792, 2304
R_COLS = 3328
E_G, E_KB, E_KA = 0, 1024, 1536
E_COLS = 1664
EQ_A, EQ_B, EQ_ROWS = 0, 512, 1024
EV_B, EV_A, EV_ROWS = 0, 512, 640
O_G, O_KM, O_KD = 0, 1024, 2048
O_COLS = 2560
OQ_M, OQ_D, OQ_ROWS = 0, 1024, 1536
OV_M, OV_D, OV_ROWS = 0, 512, 1024


def _params(sem):
    return pltpu.CompilerParams(dimension_semantics=sem, vmem_limit_bytes=VMEM_LIMIT)


def _dot(a, b):
    return jnp.dot(a, b, preferred_element_type=F32)


def _silu(z):
    return z / (1.0 + jnp.exp(-z))


def _rms(z):
    return z * lax.rsqrt(jnp.mean(z * z, axis=-1, keepdims=True) + EPS)


def _overflowed(*xs):
    chk = jnp.sum(xs[0] * 0.0)
    for x in xs[1:]:
        chk = chk + jnp.sum(x * 0.0)
    return chk != 0.0


def _ada_kernel(ct_ref, w_ref, b_ref, o_ref):
    sc = _silu(ct_ref[...])
    w = w_ref[...]
    r0 = jnp.sum(w * sc[:, 0:1], axis=0, keepdims=True)
    r1 = jnp.sum(w * sc[:, 1:2], axis=0, keepdims=True)
    o_ref[...] = jnp.concatenate([r0, r1], axis=0) + b_ref[...]


def _ada(ct, w, b):
    tn = 768
    return pl.pallas_call(
        _ada_kernel,
        out_shape=jax.ShapeDtypeStruct((2, 3 * D), F32),
        grid=(3 * D // tn,),
        in_specs=[
            pl.BlockSpec((D, 2), lambda n: (0, 0)),
            pl.BlockSpec((D, tn), lambda n: (0, n)),
            pl.BlockSpec((1, tn), lambda n: (0, n)),
        ],
        out_specs=pl.BlockSpec((2, tn), lambda n: (0, n)),
        compiler_params=_params(("parallel",)),
        name="ada_mod",
    )(ct, w, b)


def _is_latent_tile():
    return pl.program_id(0) < N_LAT_TILES


def _modulate(xin, mod_ref, ng_ref):
    mod = mod_ref[...]
    modr = jnp.where(_is_latent_tile(), mod[0:1], mod[1:2])
    shift, scale = modr[:, :D], modr[:, D:2 * D]
    return (_rms(xin) * ng_ref[...] * (1.0 + scale) + shift).astype(BF16)


def _modulated(x_ref, ctx_ref, mod_ref, ng_ref):
    return _modulate(jnp.where(_is_latent_tile(), x_ref[...], ctx_ref[...]), mod_ref, ng_ref)


def _rope_fn(rt_ref, ct_ref):
    def table(k):
        rows = jnp.concatenate([jnp.broadcast_to(rt_ref[0, Q_ROWS * k + j:Q_ROWS * k + j + 1, :], (GRID_W, LANES))
                                for j in range(Q_ROWS)], axis=0)
        return rows + jnp.concatenate([ct_ref[0, k]] * Q_ROWS, axis=0)

    cos, sa, sb = table(0), table(1), table(2)

    def rope(z):
        return z * cos + pltpu.roll(z, LANES - 16, 1) * sa + pltpu.roll(z, 16, 1) * sb

    return rope


def _tr(z):
    return z.T.astype(BF16)


def _slabs(lo, n):
    return [slice(lo + j * LANES, lo + (j + 1) * LANES) for j in range(n // LANES)]


def _proj_even_kernel(x_ref, ctx_ref, mod_ref, ng_ref, w_ref, rt_ref, ct_ref, o_ref, qt_ref, vt_ref):
    h = _modulated(x_ref, ctx_ref, mod_ref, ng_ref)
    res = _dot(h, w_ref[...])
    rope = _rope_fn(rt_ref, ct_ref)
    for src, dst in zip(_slabs(R_QA, 512) + _slabs(R_QB, 512), _slabs(EQ_A, 1024)):
        qt_ref[dst, :] = _tr(rope(res[:, src]) * LOG2E)
    for src, dst in zip(_slabs(R_KB, 512) + _slabs(R_KA, 128), _slabs(E_KB, 640)):
        o_ref[:, dst] = rope(res[:, src]).astype(BF16)
    for src, dst in zip(_slabs(R_VB, 512) + _slabs(R_VA, 128), _slabs(EV_B, 640)):
        vt_ref[0, dst, :] = _tr(res[:, src])
    o_ref[:, E_G:E_G + D] = _silu(res[:, R_G:]).astype(BF16)


def _proj_odd_kernel(x_ref, ctx_ref, ya_ref, yb_ref, ybc_ref, wout_ref, gate_ref, mod_ref, ng_ref, w_ref,
                     qg_ref, kvg_ref, wqb_ref, wkvb_ref, rt_ref, ct_ref, x1_ref, o_ref, qt_ref, vt_ref):
    is_lat = _is_latent_tile()
    y = jnp.concatenate([ya_ref[...], jnp.where(is_lat, yb_ref[...], ybc_ref[...])], axis=1)
    gate = jnp.where(is_lat, gate_ref[0:1, :], gate_ref[1:2, :])
    x1 = jnp.where(is_lat, x_ref[...], ctx_ref[...]) + gate * _dot(y, wout_ref[...])
    x1_ref[...] = x1
    h = _modulate(x1, mod_ref, ng_ref)
    res = _dot(h, w_ref[...])
    rope = _rope_fn(rt_ref, ct_ref)
    qn = (_rms(res[:, 0:256]) * qg_ref[...]).astype(BF16)
    kvn = (_rms(res[:, 256:384]) * kvg_ref[...]).astype(BF16)
    q = _dot(qn, wqb_ref[...]) * (MLA_DK ** -0.5 * LOG2E)
    kv = _dot(kvn, wkvb_ref[...])
    kpe = rope(res[:, 384:512]).astype(BF16)
    pes = (rope(q[:, 512:640]), rope(q[:, 640:768]))
    left = lax.broadcasted_iota(jnp.int32, (TM, LANES), 1) < HEAD
    for hd in range(4):
        pe = pes[hd // 2]
        pem = jnp.where(left if hd % 2 == 0 else jnp.logical_not(left), pe, 0.0)
        qt_ref[OQ_M + 256 * hd:OQ_M + 256 * hd + 128, :] = _tr(q[:, 128 * hd:128 * hd + 128])
        qt_ref[OQ_M + 256 * hd + 128:OQ_M + 256 * hd + 256, :] = _tr(pem)
        c0 = O_KM + 256 * hd
        o_ref[:, c0:c0 + 128] = kv[:, 256 * hd:256 * hd + 128].astype(BF16)
        o_ref[:, c0 + 128:c0 + 256] = kpe
        vt_ref[0, OV_M + 128 * hd:OV_M + 128 * hd + 128, :] = _tr(kv[:, 256 * hd + 128:256 * hd + 256])
    for src, dst in zip(_slabs(512, 512), _slabs(OQ_D, 512)):
        qt_ref[dst, :] = _tr(res[:, src] * LOG2E)
    o_ref[:, O_KD:O_KD + 512] = res[:, 1024:1536].astype(BF16)
    for src, dst in zip(_slabs(1536, 512), _slabs(OV_D, 512)):
        vt_ref[0, dst, :] = _tr(res[:, src])
    o_ref[:, O_G:O_G + D] = _silu(res[:, 2048:3072]).astype(BF16)


def _row_specs(n_w_cols):
    const = lambda i: (0, 0)
    return [
        pl.BlockSpec((TM, D), lambda i: (jnp.minimum(i, N_LAT_TILES - 1), 0)),
        pl.BlockSpec((L, D), const),
        pl.BlockSpec((2, 3 * D), const),
        pl.BlockSpec((1, D), const),
        pl.BlockSpec((D, n_w_cols), const),
    ]


def _table_specs():
    return [pl.BlockSpec((1, 3 * Q_ROWS, LANES), lambda i: (i, 0, 0)),
            pl.BlockSpec((1, 3, GRID_W, LANES), lambda i: (jnp.where(i < N_LAT_TILES, 0, 1), 0, 0, 0))]


def _proj_outs(n_cols, n_qt_rows, n_vt_rows):
    shapes = (jax.ShapeDtypeStruct((T, n_cols), BF16),
              jax.ShapeDtypeStruct((n_qt_rows, T), BF16),
              jax.ShapeDtypeStruct((T // KV_SLAB, n_vt_rows, KV_SLAB), BF16))
    specs = (pl.BlockSpec((TM, n_cols), lambda i: (i, 0)),
             pl.BlockSpec((n_qt_rows, TM), lambda i: (0, i)),
             pl.BlockSpec((1, n_vt_rows, KV_SLAB), lambda i: (i, 0, 0)))
    return shapes, specs


def _proj_even(x, ctx, mod, ng, w, tables):
    shapes, specs = _proj_outs(E_COLS, EQ_ROWS, EV_ROWS)
    return pl.pallas_call(
        _proj_even_kernel,
        out_shape=shapes,
        grid=(T // TM,),
        in_specs=_row_specs(R_COLS) + _table_specs(),
        out_specs=specs,
        compiler_params=_params(("parallel",)),
        name="proj_even",
    )(x, ctx, mod, ng, w, *tables)


def _proj_odd(x, ctx, ya, yb, ybc, w_out, mod_prev, mod, ng, w, qg, kvg, wqb, wkvb, tables):
    const = lambda i: (0, 0)
    lat = lambda i: (jnp.minimum(i, N_LAT_TILES - 1), 0)
    shapes, specs = _proj_outs(O_COLS, OQ_ROWS, OV_ROWS)
    row_specs = _row_specs(3072)
    return pl.pallas_call(
        _proj_odd_kernel,
        out_shape=(jax.ShapeDtypeStruct((T, D), F32),) + shapes,
        grid=(T // TM,),
        in_specs=row_specs[:2] + [
            pl.BlockSpec((TM, 512), lambda i: (i, 0)),
            pl.BlockSpec((TM, 512), lat),
            pl.BlockSpec((L, 512), const),
            pl.BlockSpec((D, D), const),
            pl.BlockSpec((2, D), lambda i: (0, 2)),
        ] + row_specs[2:] + [
            pl.BlockSpec((1, 256), const),
            pl.BlockSpec((1, 128), const),
            pl.BlockSpec((256, 768), const),
            pl.BlockSpec((128, 1024), const),
        ] + _table_specs(),
        out_specs=(pl.BlockSpec((TM, D), lambda i: (i, 0)),) + specs,
        compiler_params=_params(("parallel",)),
        name="proj_odd",
    )(x, ctx, ya, yb, ybc, w_out, mod_prev, mod, ng, w, qg, kvg, wqb, wkvb, *tables)


def _flash_kernel(*refs, n_half, n_chunks, tk, tail, unroll):
    if n_half == 2:
        qt_ref, k_ref, vt_ref, g_ref, bl_ref, sg_ref, o_ref = refs
    else:
        qt_ref, k_ref, vt_ref, g_ref, o_ref = refs
    qt = qt_ref[...]
    tq = qt.shape[1]
    if n_half == 2:
        z = jnp.zeros((HEAD, tq), BF16)
        qs = (jnp.concatenate([qt[:HEAD], z], axis=0), jnp.concatenate([z, qt[HEAD:]], axis=0))
    else:
        qs = (qt,)

    def update(r0, n, carry, exact_max):
        k = k_ref[pl.ds(r0, n), :]
        s0 = r0 // KV_SLAB
        vt = jnp.concatenate([vt_ref[s0 + j] for j in range(n // KV_SLAB)], axis=1)
        new = []
        for t in range(n_half):
            m_prev, l_prev, acc = carry[t]
            st = _dot(k, qs[t])
            if exact_max:
                m_new = jnp.maximum(m_prev, jnp.max(st, axis=0, keepdims=True))
                alpha = jnp.exp2(m_prev - m_new)
                pt = jnp.exp2(st - m_new)
                l_new = alpha * l_prev + jnp.sum(pt, axis=0, keepdims=True)
                acc = alpha * acc + _dot(vt, pt.astype(BF16))
            else:
                m_new = m_prev
                pt = jnp.exp2(st - m_prev)
                l_new = l_prev + jnp.sum(pt, axis=0, keepdims=True)
                acc = acc + _dot(vt, pt.astype(BF16))
            new.append((m_new, l_new, acc))
        return tuple(new)

    carry = tuple((jnp.full((1, tq), NEG, F32), jnp.zeros((1, tq), F32), jnp.zeros((LANES, tq), F32))
                  for _ in range(n_half))
    if tail:
        carry = update(n_chunks * tk, tail, carry, True)
    if n_chunks == 1:
        carry = update(0, tk, carry, not tail)
    elif n_chunks > 1:
        def walk(exact_max, unroll_):
            return lambda cr: lax.fori_loop(
                0, n_chunks, lambda c, cr_: update(pl.multiple_of(c * tk, tk), tk, cr_, exact_max), cr,
                unroll=unroll_)

        seeded = carry
        carry = walk(False, unroll)(seeded)
        carry = lax.cond(_overflowed(*[x for cr in carry for x in cr[1:]]),
                         lambda _: walk(True, 1)(seeded), lambda cr: cr, carry)
    if n_half == 2:
        bl = bl_ref[...]
        lam = (jnp.exp(jnp.sum(bl[0:1] * bl[1:2], axis=1, keepdims=True))
               - jnp.exp(jnp.sum(bl[2:3] * bl[3:4], axis=1, keepdims=True)) + LAM_INIT)
        ot = carry[0][2] / carry[0][1] - lam * (carry[1][2] / carry[1][1])
        o = _rms(ot.T) * sg_ref[...] * (1.0 - LAM_INIT)
    else:
        o = (carry[0][2] / carry[0][1]).T
    o_ref[...] = (o * g_ref[...].astype(F32)).astype(BF16)


def _flash(qt, proj, vt, extra, *, n_half, dq, qrow, kcol, vrow, gcol, n_heads, q_row0, n_q_rows, kv_row0,
           n_kv_rows, tq, tk, unroll, name):
    nq, n_chunks = n_q_rows // tq, n_kv_rows // tk
    tail = n_kv_rows - n_chunks * tk
    qb0, kb0 = q_row0 // tq, kv_row0 // n_kv_rows
    in_specs = [
        pl.BlockSpec((dq, tq), lambda h, i: (qrow // dq + h, qb0 + i)),
        pl.BlockSpec((n_kv_rows, dq), lambda h, i: (kb0, kcol // dq + h), pipeline_mode=pl.Buffered(1)),
        pl.BlockSpec((n_kv_rows // KV_SLAB, LANES, KV_SLAB), lambda h, i: (kb0, vrow // LANES + h, 0),
                     pipeline_mode=pl.Buffered(1)),
        pl.BlockSpec((tq, LANES), lambda h, i: (qb0 + i, gcol // LANES + h)),
    ]
    args = [qt, proj, vt, proj]
    if n_half == 2:
        in_specs += [pl.BlockSpec((4, HEAD), lambda h, i: (0, 0)),
                     pl.BlockSpec((1, LANES), lambda h, i: (0, 0))]
        args += list(extra)
    return pl.pallas_call(
        functools.partial(_flash_kernel, n_half=n_half, n_chunks=n_chunks, tk=tk, tail=tail, unroll=unroll),
        out_shape=jax.ShapeDtypeStruct((n_q_rows, n_heads * LANES), BF16),
        grid=(n_heads, nq),
        in_specs=in_specs,
        out_specs=pl.BlockSpec((tq, LANES), lambda h, i: (i, h)),
        compiler_params=_params(("parallel", "parallel")),
        name=name,
    )(*args)


def _zero_padded(qh, upper):
    z = jnp.zeros_like(qh)
    return jnp.concatenate([qh, z] if upper else [z, qh], axis=0)


def _store_head_pair(o_ref, g_ref, slab, top, bottom):
    o = jnp.concatenate([top, bottom], axis=0).T
    c = slice(slab * LANES, (slab + 1) * LANES)
    o_ref[:, c] = (o * g_ref[:, c].astype(F32)).astype(BF16)


def _local_softmax_pv(st_c, st_l, vt_c, vt_l, floor, exact_max):
    m = jnp.max(st_c, axis=0, keepdims=True)
    if floor is not None:
        m = jnp.maximum(m, floor)
    if exact_max:
        m = jnp.maximum(m, jnp.max(st_l, axis=0, keepdims=True))
    pt_c = jnp.exp2(st_c - m)
    pt_l = jnp.exp2(st_l - m)
    denom = jnp.sum(pt_c, axis=0, keepdims=True) + jnp.sum(pt_l, axis=0, keepdims=True)
    if floor is not None:
        denom = denom + jnp.exp2(floor - m)
    pv = _dot(vt_c, pt_c.astype(BF16)) + _dot(vt_l, pt_l.astype(BF16))
    return pv * (1.0 / denom), _overflowed(denom, pv)


def _streamed_then_exact(run):
    @pl.when(run(False))
    def _():
        run(True)


def _mixer_a_kernel(sink_ref, qt_ref, k0_ref, k1_ref, k2_ref, k3_ref, kc_ref, v0_ref, v1_ref, v2_ref, vc_ref,
                    g_ref, o_ref):
    i = pl.program_id(0)
    n_loc = TM + 2 * A_WINDOW
    k_l = jnp.concatenate([k0_ref[...], k1_ref[...], k2_ref[...], k3_ref[...]], axis=0)
    k_c = kc_ref[...]
    vt_l = jnp.concatenate([v0_ref[0][:, TM - A_WINDOW:], v1_ref[0], v2_ref[0][:, :A_WINDOW]], axis=1)
    vt_c = vc_ref[0]
    kl = lax.broadcasted_iota(jnp.int32, (n_loc, TM), 0)
    ql = lax.broadcasted_iota(jnp.int32, (n_loc, TM), 1)
    kpos = i * TM - A_WINDOW + kl
    dist = kl - ql
    ok = (jnp.where(dist >= 0, 1, 0) * jnp.where(dist <= 2 * A_WINDOW, 1, 0) * jnp.where(kpos >= 0, 1, 0)
          * jnp.where(kpos < S, 1, 0) * jnp.where(i < N_LAT_TILES, 1, 0))
    mask = jnp.where(ok > 0, 0.0, NEG)
    mask4 = jnp.concatenate([mask] * 4, axis=1)
    qt = qt_ref[...]

    def run(exact_max):
        bad = None
        for kh in range(2):
            heads = range(4 * kh, 4 * kh + 4)
            w = jnp.concatenate([_zero_padded(qt[HEAD * hq:HEAD * (hq + 1)], kh == 0) for hq in heads], axis=1)
            snk = jnp.concatenate([jnp.full((1, TM), sink_ref[hq] * LOG2E, F32) for hq in heads], axis=1)
            ot, overflow = _local_softmax_pv(_dot(k_c, w), _dot(k_l, w) + mask4, vt_c, vt_l, snk, exact_max)
            ot = ot[HEAD * kh:HEAD * (kh + 1)]
            for pair in range(2):
                _store_head_pair(o_ref, g_ref, 2 * kh + pair, ot[:, 2 * pair * TM:(2 * pair + 1) * TM],
                                 ot[:, (2 * pair + 1) * TM:(2 * pair + 2) * TM])
            bad = overflow if bad is None else jnp.logical_or(bad, overflow)
        return bad

    _streamed_then_exact(run)


def _mixer_a(qt, proj, vt, sink):
    nt = T // TM
    last_blk = T // A_WINDOW - 1

    def k_spec(off):
        return pl.BlockSpec((A_WINDOW, LANES),
                            lambda i: (jnp.clip(2 * i + off, 0, last_blk), E_KA // LANES))

    def v_spec(off):
        return pl.BlockSpec((1, LANES, KV_SLAB), lambda i: (jnp.clip(i + off, 0, nt - 1), EV_A // LANES, 0))

    return pl.pallas_call(
        _mixer_a_kernel,
        out_shape=jax.ShapeDtypeStruct((T, 512), BF16),
        grid=(nt,),
        in_specs=[pl.BlockSpec(memory_space=pltpu.SMEM),
                  pl.BlockSpec((512, TM), lambda i: (EQ_A // 512, i))]
        + [k_spec(off) for off in (-1, 0, 1, 2)]
        + [pl.BlockSpec((L, LANES), lambda i: (CTX_TILE, E_KA // LANES))]
        + [v_spec(off) for off in (-1, 0, 1)]
        + [pl.BlockSpec((1, LANES, KV_SLAB), lambda i: (CTX_TILE, EV_A // LANES, 0)),
           pl.BlockSpec((TM, 512), lambda i: (i, E_G // 512))],
        out_specs=pl.BlockSpec((TM, 512), lambda i: (i, 0)),
        compiler_params=_params(("parallel",)),
        name="mixer_a",
    )(sink, qt, *([proj] * 5), *([vt] * 4), proj)


def _mixer_d_kernel(qt_ref, k0_ref, k1_ref, k2_ref, kc_ref, v0_ref, v1_ref, v2_ref, vc_ref, bias_ref, g_ref, o_ref):
    k_l = jnp.concatenate([k0_ref[...], k1_ref[...], k2_ref[...]], axis=0)
    k_c = kc_ref[...]
    vt_l = jnp.concatenate([v0_ref[0], v1_ref[0], v2_ref[0]], axis=1)
    vt_c = vc_ref[0]
    qt = qt_ref[...]

    def run(exact_max):
        bad = None
        for pair in range(4):
            rows = slice(pair * LANES, (pair + 1) * LANES)
            w = jnp.concatenate([_zero_padded(qt[rows][:HEAD], True), _zero_padded(qt[rows][HEAD:], False)],
                                axis=1)
            bias = jnp.concatenate([bias_ref[0, 2 * pair], bias_ref[0, 2 * pair + 1]], axis=1)
            ot, overflow = _local_softmax_pv(_dot(k_c[:, rows], w), _dot(k_l[:, rows], w) + bias, vt_c[rows],
                                         vt_l[rows], None, exact_max)
            _store_head_pair(o_ref, g_ref, pair, ot[:HEAD, :TM], ot[HEAD:, TM:])
            bad = overflow if bad is None else jnp.logical_or(bad, overflow)
        return bad

    _streamed_then_exact(run)


def _mixer_d(qt, proj, vt, bias):
    nt = S // TM

    def k_spec(off):
        return pl.BlockSpec((TM, 512), lambda i: (jnp.clip(i + off, 0, nt - 1), O_KD // 512))

    def v_spec(off):
        return pl.BlockSpec((1, 512, KV_SLAB), lambda i: (jnp.clip(i + off, 0, nt - 1), OV_D // 512, 0))

    def variant(i):
        return jnp.where(i == 0, 0, jnp.where(i == nt - 1, 2, 1))

    return pl.pallas_call(
        _mixer_d_kernel,
        out_shape=jax.ShapeDtypeStruct((S, 512), BF16),
        grid=(nt,),
        in_specs=[pl.BlockSpec((512, TM), lambda i: (OQ_D // 512, i))]
        + [k_spec(off) for off in (-1, 0, 1)]
        + [pl.BlockSpec((L, 512), lambda i: (CTX_TILE, O_KD // 512))]
        + [v_spec(off) for off in (-1, 0, 1)]
        + [pl.BlockSpec((1, 512, KV_SLAB), lambda i: (CTX_TILE, OV_D // 512, 0)),
           pl.BlockSpec((1, 8, 3 * TM, TM), lambda i: (variant(i), 0, 0, 0)),
           pl.BlockSpec((TM, 512), lambda i: (i, O_G // 512 + 1))],
        out_specs=pl.BlockSpec((TM, 512), lambda i: (i, 0)),
        compiler_params=_params(("parallel",)),
        name="mixer_d",
    )(qt, *([proj] * 4), *([vt] * 4), bias, proj)


D_ROWS, D_COLS = 8, 16
RPB_R, RPB_C = 2 * D_ROWS - 1, 2 * D_COLS - 1
Q_GRID_ROWS = Q_ROWS
K_GRID_ROWS = 3 * Q_GRID_ROWS


def _bias_kernel(rpb_ref, o_ref):
    base = pl.program_id(0) * (RPB_R * RPB_C)
    kc = lax.broadcasted_iota(jnp.int32, (GRID_W, LANES), 0)
    lane = lax.broadcasted_iota(jnp.int32, (GRID_W, LANES), 1)
    right = lane >= GRID_W
    qc = jnp.where(right, lane - GRID_W, lane)
    cs = jnp.clip(qc - D_COLS // 2, 0, GRID_W - D_COLS)
    col_ok = jnp.where(kc >= cs, 1, 0) * jnp.where(kc < cs + D_COLS, 1, 0)
    dc = jnp.where(col_ok > 0, kc - qc + (D_COLS - 1), -1)
    dc_l = jnp.where(right, -1, dc)
    dc_r = jnp.where(right, dc, -1)
    neg = jnp.full((GRID_W, LANES), NEG, F32)
    pairs = []
    for d in range(RPB_R - 1):
        t2 = neg
        for j in range(RPB_C):
            t2 = jnp.where(dc_l == j, rpb_ref[base + (d + 1) * RPB_C + j] * LOG2E, t2)
            t2 = jnp.where(dc_r == j, rpb_ref[base + d * RPB_C + j] * LOG2E, t2)
        pairs.append(t2)
    row_ok = (lambda ri, kr: kr >= Q_GRID_ROWS,
              lambda ri, kr: 0 <= kr - ri < D_ROWS,
              lambda ri, kr: kr < D_ROWS)
    for var in range(3):
        for kr in range(K_GRID_ROWS):
            for b in range(Q_GRID_ROWS // 2):
                blk = pairs[kr - 2 * b + 2]
                ok_l, ok_r = row_ok[var](2 * b, kr), row_ok[var](2 * b + 1, kr)
                if not ok_l and not ok_r:
                    blk = neg
                elif not ok_l:
                    blk = jnp.where(right, blk, NEG)
                elif not ok_r:
                    blk = jnp.where(right, NEG, blk)
                o_ref[var, 0, kr * GRID_W:(kr + 1) * GRID_W, b * LANES:(b + 1) * LANES] = blk


def _neighbourhood_bias(rpb):
    n_heads = rpb.shape[0]
    return pl.pallas_call(
        _bias_kernel,
        out_shape=jax.ShapeDtypeStruct((3, n_heads, 3 * TM, TM), F32),
        grid=(n_heads,),
        in_specs=[pl.BlockSpec(memory_space=pltpu.SMEM)],
        out_specs=pl.BlockSpec((3, 1, 3 * TM, TM), lambda h: (0, h, 0, 0)),
        compiler_params=_params(("parallel",)),
        name="nbr_bias",
    )(rpb.reshape(-1))


def _final_kernel(x_ref, y1_ref, y2_ref, w_ref, gate_ref, fg_ref, o_ref):
    y = jnp.concatenate([y1_ref[...], y2_ref[...]], axis=1)
    xn = x_ref[...] + gate_ref[0:1, :] * _dot(y, w_ref[...])
    o_ref[...] = _rms(xn) * fg_ref[...]


def _final_proj(x, y1, y2, w, mod, final_g):
    const = lambda i: (0, 0)
    return pl.pallas_call(
        _final_kernel,
        out_shape=jax.ShapeDtypeStruct((S, D), F32),
        grid=(S // OUT_TM,),
        in_specs=[
            pl.BlockSpec((OUT_TM, D), lambda i: (i, 0)),
            pl.BlockSpec((OUT_TM, 512), lambda i: (i, 0)),
            pl.BlockSpec((OUT_TM, 512), lambda i: (i, 0)),
            pl.BlockSpec((D, D), const),
            pl.BlockSpec((2, D), lambda i: (0, 2)),
            pl.BlockSpec((1, D), const),
        ],
        out_specs=pl.BlockSpec((OUT_TM, D), lambda i: (i, 0)),
        compiler_params=_params(("parallel",)),
        name="final_proj",
    )(x, y1, y2, w, mod, final_g)


def _rope_tables():
    quarter = HEAD // 4
    inv = ROPE_THETA ** (-jnp.arange(quarter, dtype=F32) / quarter)

    def parts(n):
        ang = jnp.arange(n).astype(F32)[:, None] * inv[None, :]
        cos, sin, z = jnp.cos(ang), jnp.sin(ang), jnp.zeros((n, quarter), F32)
        return (jnp.concatenate([cos, cos], axis=1), jnp.concatenate([-sin, z], axis=1),
                jnp.concatenate([z, sin], axis=1))

    def lanes(first32, second32):
        return jnp.tile(jnp.concatenate([first32, second32], axis=1), (1, LANES // HEAD))

    n_rows = S // GRID_W
    zr, zc = jnp.zeros((n_rows, 32), F32), jnp.zeros((GRID_W, 32), F32)
    ident = (jnp.ones((Q_ROWS, 32), F32), jnp.zeros((Q_ROWS, 32), F32), jnp.zeros((Q_ROWS, 32), F32))
    rt = jnp.concatenate([
        jnp.concatenate([lanes(p, zr).reshape(N_LAT_TILES, Q_ROWS, LANES) for p in parts(n_rows)], axis=1),
        jnp.concatenate([lanes(p, jnp.zeros((Q_ROWS, 32), F32)) for p in ident], axis=0)[None]], axis=0)
    ct_lat = jnp.stack([lanes(zc, p) for p in parts(GRID_W)])
    ct_ctx = jnp.stack([lanes(zc, jnp.full((GRID_W, 32), v, F32)) for v in (1.0, 0.0, 0.0)])
    return rt, jnp.stack([ct_lat, ct_ctx])


def kernel(x, c, ctx, c_ctx, ev_norm_g, ev_w_ada, ev_b_ada, ev_w_in, ev_a_sink, ev_b_lambda, ev_b_subln_g, ev_w_out, od_norm_g, od_w_ada, od_b_ada, od_w_in, od_c_q_norm_g, od_c_kv_norm_g, od_c_w_qb, od_c_w_kvb, od_d_rpb, od_w_out, final_norm_g):
    assert x.shape == (1, S, D) and ctx.shape == (1, L, D)
    assert ev_w_in.shape[0] == 1 and od_w_in.shape[0] == 1
    x0, xc0 = x[0], ctx[0]
    ct = jnp.stack([c[0], c_ctx], axis=1)
    tables = _rope_tables()
    qk_scale = HEAD ** -0.5

    mod = _ada(ct, ev_w_ada[0], ev_b_ada[0][None])
    w = ev_w_in[0]
    w_e = jnp.concatenate([w[:, :R_KA] * qk_scale, w[:, R_KA:R_QB], w[:, R_QB:R_KB] * qk_scale, w[:, R_KB:]],
                          axis=1).astype(BF16)
    proj, qt, vt = _proj_even(x0, xc0, mod, ev_norm_g[0][None], w_e, tables)
    ya = _mixer_a(qt, proj, vt, ev_a_sink[0])
    diff_extra = (ev_b_lambda[0], ev_b_subln_g[0][None])
    diff = functools.partial(_flash, qt, proj, vt, diff_extra, n_half=2, dq=128, qrow=EQ_B, kcol=E_KB, vrow=EV_B,
                             gcol=E_G + 512, n_heads=4)
    yb = diff(q_row0=0, n_q_rows=S, kv_row0=0, n_kv_rows=T, tq=FLASH_TQ, tk=FLASH_TK, unroll=FLASH_UNROLL,
              name="diff_attn")
    ybc = diff(q_row0=S, n_q_rows=L, kv_row0=S, n_kv_rows=L, tq=L, tk=L, unroll=1, name="diff_attn_ctx")
    mod_prev = mod

    mod = _ada(ct, od_w_ada[0], od_b_ada[0][None])
    w = od_w_in[0]
    w_o = jnp.concatenate([w[:, 0:384], w[:, 384:448], w[:, 384:448], w[:, 448:960] * qk_scale, w[:, 960:]],
                          axis=1).astype(BF16)
    wqb = od_c_w_qb[0].reshape(256, 4, MLA_DK)
    wqb = jnp.concatenate([wqb[:, :, :128].reshape(256, 512), wqb[:, :, 128:].reshape(256, 256)],
                          axis=1).astype(BF16)
    x1, proj, qt, vt = _proj_odd(x0, xc0, ya, yb, ybc, ev_w_out[0].astype(BF16), mod_prev, mod, od_norm_g[0][None],
                                 w_o, od_c_q_norm_g[0][None], od_c_kv_norm_g[0][None], wqb,
                                 od_c_w_kvb[0].astype(BF16), tables)
    ym = _flash(qt, proj, vt, (), n_half=1, dq=256, qrow=OQ_M, kcol=O_KM, vrow=OV_M, gcol=O_G, n_heads=4,
                q_row0=0, n_q_rows=S, kv_row0=0, n_kv_rows=T, tq=FLASH_TQ, tk=FLASH_TK, unroll=FLASH_UNROLL,
                name="mla_attn")
    yd = _mixer_d(qt, proj, vt, _neighbourhood_bias(od_d_rpb[0]))
    out = _final_proj(x1, ym, yd, od_w_out[0].astype(BF16), mod, final_norm_g[None])
    return out[None]
```

```python
import functools
import math

import jax
import jax.numpy as jnp
from jax import lax
from jax.experimental import pallas as pl
from jax.experimental.pallas import tpu as pltpu

F32 = jnp.float32
BF16 = jnp.bfloat16

D = 1024
S = 16384
L = 256
T = S + L
GRID_W = 64
EPS = 1e-6
NEG = -1e30
ROPE_THETA = 10000.0
A_WINDOW = 128
LANES = 128
HEAD = 64
MLA_DK = 192
LAM_INIT = 0.8 - 0.6 * math.exp(-0.3 * 0)

LOG2E = math.log2(math.e)
FLASH_TQ = 512
FLASH_TK = 1024

TM = 256
Q_ROWS = TM // GRID_W
OUT_TM = 512
N_LAT_TILES = S // TM
CTX_TILE = S // TM
VMEM_LIMIT = 56 * 1024 * 1024
KV_SLAB = 256

R_QA, R_KA, R_VA, R_QB, R_KB, R_VB, R_G = 0, 512, 640, 768, 1280, 1792, 2304
R_COLS = 3328
E_G, E_KB, E_KA = 0, 1024, 1536
E_COLS = 1664
EQ_A, EQ_B, EQ_ROWS = 0, 512, 1024
EV_B, EV_A, EV_ROWS = 0, 512, 640
O_G, O_KM, O_KD = 0, 1024, 2048
O_COLS = 2560
OQ_M, OQ_D, OQ_ROWS = 0, 1024, 1536
OV_M, OV_D, OV_ROWS = 0, 512, 1024


def _params(sem):
    return pltpu.CompilerParams(dimension_semantics=sem, vmem_limit_bytes=VMEM_LIMIT)


def _dot(a, b):
    return jnp.dot(a, b, preferred_element_type=F32)


def _silu(z):
    return z / (1.0 + jnp.exp(-z))


def _rms(z):
    return z * lax.rsqrt(jnp.mean(z * z, axis=-1, keepdims=True) + EPS)


def _overflowed(*xs):
    chk = jnp.sum(xs[0] * 0.0)
    for x in xs[1:]:
        chk = chk + jnp.sum(x * 0.0)
    return chk != 0.0


def _ada_kernel(ct_ref, w_ref, b_ref, o_ref):
    sc = _silu(ct_ref[...])
    w = w_ref[...]
    r0 = jnp.sum(w * sc[:, 0:1], axis=0, keepdims=True)
    r1 = jnp.sum(w * sc[:, 1:2], axis=0, keepdims=True)
    o_ref[...] = jnp.concatenate([r0, r1], axis=0) + b_ref[...]


def _ada(ct, w, b):
    tn = 768
    return pl.pallas_call(
        _ada_kernel,
        out_shape=jax.ShapeDtypeStruct((2, 3 * D), F32),
        grid=(3 * D // tn,),
        in_specs=[
            pl.BlockSpec((D, 2), lambda n: (0, 0)),
            pl.BlockSpec((D, tn), lambda n: (0, n)),
            pl.BlockSpec((1, tn), lambda n: (0, n)),
        ],
        out_specs=pl.BlockSpec((2, tn), lambda n: (0, n)),
        compiler_params=_params(("parallel",)),
        name="ada_mod",
    )(ct, w, b)


def _is_latent_tile():
    return pl.program_id(0) < N_LAT_TILES


def _modulate(xin, mod_ref, ng_ref):
    mod = mod_ref[...]
    modr = jnp.where(_is_latent_tile(), mod[0:1], mod[1:2])
    shift, scale = modr[:, :D], modr[:, D:2 * D]
    return (_rms(xin) * ng_ref[...] * (1.0 + scale) + shift).astype(BF16)


def _modulated(x_ref, ctx_ref, mod_ref, ng_ref):
    return _modulate(jnp.where(_is_latent_tile(), x_ref[...], ctx_ref[...]), mod_ref, ng_ref)


def _rope_fn(rt_ref, ct_ref):
    def table(k):
        rows = jnp.concatenate([jnp.broadcast_to(rt_ref[0, Q_ROWS * k + j:Q_ROWS * k + j + 1, :], (GRID_W, LANES))
                                for j in range(Q_ROWS)], axis=0)
        return rows + jnp.concatenate([ct_ref[0, k]] * Q_ROWS, axis=0)

    cos, sa, sb = table(0), table(1), table(2)

    def rope(z):
        return z * cos + pltpu.roll(z, LANES - 16, 1) * sa + pltpu.roll(z, 16, 1) * sb

    return rope


def _tr(z):
    return z.T.astype(BF16)


def _slabs(lo, n):
    return [slice(lo + j * LANES, lo + (j + 1) * LANES) for j in range(n // LANES)]


def _proj_even_kernel(x_ref, ctx_ref, mod_ref, ng_ref, w_ref, rt_ref, ct_ref, o_ref, qt_ref, vt_ref):
    h = _modulated(x_ref, ctx_ref, mod_ref, ng_ref)
    res = _dot(h, w_ref[...])
    rope = _rope_fn(rt_ref, ct_ref)
    for src, dst in zip(_slabs(R_QA, 512) + _slabs(R_QB, 512), _slabs(EQ_A, 1024)):
        qt_ref[dst, :] = _tr(rope(res[:, src]) * LOG2E)
    for src, dst in zip(_slabs(R_KB, 512) + _slabs(R_KA, 128), _slabs(E_KB, 640)):
        o_ref[:, dst] = rope(res[:, src]).astype(BF16)
    for src, dst in zip(_slabs(R_VB, 512) + _slabs(R_VA, 128), _slabs(EV_B, 640)):
        vt_ref[0, dst, :] = _tr(res[:, src])
    o_ref[:, E_G:E_G + D] = _silu(res[:, R_G:]).astype(BF16)


def _proj_odd_kernel(x_ref, ctx_ref, ya_ref, yb_ref, ybc_ref, wout_ref, gate_ref, mod_ref, ng_ref, w_ref,
                     qg_ref, kvg_ref, wqb_ref, wkvb_ref, rt_ref, ct_ref, x1_ref, o_ref, qt_ref, vt_ref):
    is_lat = _is_latent_tile()
    y = jnp.concatenate([ya_ref[...], jnp.where(is_lat, yb_ref[...], ybc_ref[...])], axis=1)
    gate = jnp.where(is_lat, gate_ref[0:1, :], gate_ref[1:2, :])
    x1 = jnp.where(is_lat, x_ref[...], ctx_ref[...]) + gate * _dot(y, wout_ref[...])
    x1_ref[...] = x1
    h = _modulate(x1, mod_ref, ng_ref)
    res = _dot(h, w_ref[...])
    rope = _rope_fn(rt_ref, ct_ref)
    qn = (_rms(res[:, 0:256]) * qg_ref[...]).astype(BF16)
    kvn = (_rms(res[:, 256:384]) * kvg_ref[...]).astype(BF16)
    q = _dot(qn, wqb_ref[...]) * (MLA_DK ** -0.5 * LOG2E)
    kv = _dot(kvn, wkvb_ref[...])
    kpe = rope(res[:, 384:512]).astype(BF16)
    pes = (rope(q[:, 512:640]), rope(q[:, 640:768]))
    left = lax.broadcasted_iota(jnp.int32, (TM, LANES), 1) < HEAD
    for hd in range(4):
        pe = pes[hd // 2]
        pem = jnp.where(left if hd % 2 == 0 else jnp.logical_not(left), pe, 0.0)
        qt_ref[OQ_M + 256 * hd:OQ_M + 256 * hd + 128, :] = _tr(q[:, 128 * hd:128 * hd + 128])
        qt_ref[OQ_M + 256 * hd + 128:OQ_M + 256 * hd + 256, :] = _tr(pem)
        c0 = O_KM + 256 * hd
        o_ref[:, c0:c0 + 128] = kv[:, 256 * hd:256 * hd + 128].astype(BF16)
        o_ref[:, c0 + 128:c0 + 256] = kpe
        vt_ref[0, OV_M + 128 * hd:OV_M + 128 * hd + 128, :] = _tr(kv[:, 256 * hd + 128:256 * hd + 256])
    for src, dst in zip(_slabs(512, 512), _slabs(OQ_D, 512)):
        qt_ref[dst, :] = _tr(res[:, src] * LOG2E)
    o_ref[:, O_KD:O_KD + 512] = res[:, 1024:1536].astype(BF16)
    for src, dst in zip(_slabs(1536, 512), _slabs(OV_D, 512)):
        vt_ref[0, dst, :] = _tr(res[:, src])
    o_ref[:, O_G:O_G + D] = _silu(res[:, 2048:3072]).astype(BF16)


def _row_specs(n_w_cols):
    const = lambda i: (0, 0)
    return [
        pl.BlockSpec((TM, D), lambda i: (jnp.minimum(i, N_LAT_TILES - 1), 0)),
        pl.BlockSpec((L, D), const),
        pl.BlockSpec((2, 3 * D), const),
        pl.BlockSpec((1, D), const),
        pl.BlockSpec((D, n_w_cols), const),
    ]


def _table_specs():
    return [pl.BlockSpec((1, 3 * Q_ROWS, LANES), lambda i: (i, 0, 0)),
            pl.BlockSpec((1, 3, GRID_W, LANES), lambda i: (jnp.where(i < N_LAT_TILES, 0, 1), 0, 0, 0))]


def _proj_outs(n_cols, n_qt_rows, n_vt_rows):
    shapes = (jax.ShapeDtypeStruct((T, n_cols), BF16),
              jax.ShapeDtypeStruct((n_qt_rows, T), BF16),
              jax.ShapeDtypeStruct((T // KV_SLAB, n_vt_rows, KV_SLAB), BF16))
    specs = (pl.BlockSpec((TM, n_cols), lambda i: (i, 0)),
             pl.BlockSpec((n_qt_rows, TM), lambda i: (0, i)),
             pl.BlockSpec((1, n_vt_rows, KV_SLAB), lambda i: (i, 0, 0)))
    return shapes, specs


def _proj_even(x, ctx, mod, ng, w, tables):
    shapes, specs = _proj_outs(E_COLS, EQ_ROWS, EV_ROWS)
    return pl.pallas_call(
        _proj_even_kernel,
        out_shape=shapes,
        grid=(T // TM,),
        in_specs=_row_specs(R_COLS) + _table_specs(),
        out_specs=specs,
        compiler_params=_params(("parallel",)),
        name="proj_even",
    )(x, ctx, mod, ng, w, *tables)


def _proj_odd(x, ctx, ya, yb, ybc, w_out, mod_prev, mod, ng, w, qg, kvg, wqb, wkvb, tables):
    const = lambda i: (0, 0)
    lat = lambda i: (jnp.minimum(i, N_LAT_TILES - 1), 0)
    shapes, specs = _proj_outs(O_COLS, OQ_ROWS, OV_ROWS)
    row_specs = _row_specs(3072)
    return pl.pallas_call(
        _proj_odd_kernel,
        out_shape=(jax.ShapeDtypeStruct((T, D), F32),) + shapes,
        grid=(T // TM,),
        in_specs=row_specs[:2] + [
            pl.BlockSpec((TM, 512), lambda i: (i, 0)),
            pl.BlockSpec((TM, 512), lat),
            pl.BlockSpec((L, 512), const),
            pl.BlockSpec((D, D), const),
            pl.BlockSpec((2, D), lambda i: (0, 2)),
        ] + row_specs[2:] + [
            pl.BlockSpec((1, 256), const),
            pl.BlockSpec((1, 128), const),
            pl.BlockSpec((256, 768), const),
            pl.BlockSpec((128, 1024), const),
        ] + _table_specs(),
        out_specs=(pl.BlockSpec((TM, D), lambda i: (i, 0)),) + specs,
        compiler_params=_params(("parallel",)),
        name="proj_odd",
    )(x, ctx, ya, yb, ybc, w_out, mod_prev, mod, ng, w, qg, kvg, wqb, wkvb, *tables)


def _flash_kernel(*refs, n_half, n_chunks, tk, tail):
    if n_half == 2:
        qt_ref, k_ref, vt_ref, g_ref, bl_ref, sg_ref, o_ref = refs
    else:
        qt_ref, k_ref, vt_ref, g_ref, o_ref = refs
    qt = qt_ref[...]
    tq = qt.shape[1]
    if n_half == 2:
        z = jnp.zeros((HEAD, tq), BF16)
        qs = (jnp.concatenate([qt[:HEAD], z], axis=0), jnp.concatenate([z, qt[HEAD:]], axis=0))
    else:
        qs = (qt,)

    def scores(r0, n):
        k = k_ref[pl.ds(r0, n), :]
        s0 = r0 // KV_SLAB
        vt = jnp.concatenate([vt_ref[s0 + j] for j in range(n // KV_SLAB)], axis=1)
        return [_dot(k, q) for q in qs], vt

    def accumulate(sts, vt, carry, exact_max):
        new = []
        for t in range(n_half):
            m_prev, l_prev, acc = carry[t]
            st = sts[t]
            if exact_max:
                m_new = jnp.maximum(m_prev, jnp.max(st, axis=0, keepdims=True))
                alpha = jnp.exp2(m_prev - m_new)
                pt = jnp.exp2(st - m_new)
                l_new = alpha * l_prev + jnp.sum(pt, axis=0, keepdims=True)
                acc = alpha * acc + _dot(vt, pt.astype(BF16))
            else:
                m_new = m_prev
                pt = jnp.exp2(st - m_prev)
                l_new = l_prev + jnp.sum(pt, axis=0, keepdims=True)
                acc = acc + _dot(vt, pt.astype(BF16))
            new.append((m_new, l_new, acc))
        return tuple(new)

    carry = tuple((jnp.full((1, tq), NEG, F32), jnp.zeros((1, tq), F32), jnp.zeros((LANES, tq), F32))
                  for _ in range(n_half))
    if tail:
        carry = accumulate(*scores(n_chunks * tk, tail), carry, True)
    if n_chunks == 1:
        carry = accumulate(*scores(0, tk), carry, not tail)
    elif n_chunks > 1:
        seeded = carry
        ahead = scores(0, tk)
        for c in range(n_chunks):
            cur, ahead = ahead, (scores((c + 1) * tk, tk) if c + 1 < n_chunks else None)
            carry = accumulate(*cur, carry, False)

        def redo(_):
            return lax.fori_loop(
                0, n_chunks, lambda c, cr: accumulate(*scores(pl.multiple_of(c * tk, tk), tk), cr, True), seeded)

        carry = lax.cond(_overflowed(*[x for cr in carry for x in cr[1:]]), redo, lambda cr: cr, carry)
    if n_half == 2:
        bl = bl_ref[...]
        lam = (jnp.exp(jnp.sum(bl[0:1] * bl[1:2], axis=1, keepdims=True))
               - jnp.exp(jnp.sum(bl[2:3] * bl[3:4], axis=1, keepdims=True)) + LAM_INIT)
        ot = carry[0][2] / carry[0][1] - lam * (carry[1][2] / carry[1][1])
        o = _rms(ot.T) * sg_ref[...] * (1.0 - LAM_INIT)
    else:
        o = (carry[0][2] / carry[0][1]).T
    o_ref[...] = (o * g_ref[...].astype(F32)).astype(BF16)


def _flash(qt, proj, vt, extra, *, n_half, dq, qrow, kcol, vrow, gcol, n_heads, q_row0, n_q_rows, kv_row0,
           n_kv_rows, tq, tk, name):
    nq, n_chunks = n_q_rows // tq, n_kv_rows // tk
    tail = n_kv_rows - n_chunks * tk
    qb0, kb0 = q_row0 // tq, kv_row0 // n_kv_rows
    in_specs = [
        pl.BlockSpec((dq, tq), lambda h, i: (qrow // dq + h, qb0 + i)),
        pl.BlockSpec((n_kv_rows, dq), lambda h, i: (kb0, kcol // dq + h), pipeline_mode=pl.Buffered(1)),
        pl.BlockSpec((n_kv_rows // KV_SLAB, LANES, KV_SLAB), lambda h, i: (kb0, vrow // LANES + h, 0),
                     pipeline_mode=pl.Buffered(1)),
        pl.BlockSpec((tq, LANES), lambda h, i: (qb0 + i, gcol // LANES + h)),
    ]
    args = [qt, proj, vt, proj]
    if n_half == 2:
        in_specs += [pl.BlockSpec((4, HEAD), lambda h, i: (0, 0)),
                     pl.BlockSpec((1, LANES), lambda h, i: (0, 0))]
        args += list(extra)
    return pl.pallas_call(
        functools.partial(_flash_kernel, n_half=n_half, n_chunks=n_chunks, tk=tk, tail=tail),
        out_shape=jax.ShapeDtypeStruct((n_q_rows, n_heads * LANES), BF16),
        grid=(n_heads, nq),
        in_specs=in_specs,
        out_specs=pl.BlockSpec((tq, LANES), lambda h, i: (i, h)),
        compiler_params=_params(("parallel", "parallel")),
        name=name,
    )(*args)


def _zero_padded(qh, upper):
    z = jnp.zeros_like(qh)
    return jnp.concatenate([qh, z] if upper else [z, qh], axis=0)


def _store_head_pair(o_ref, g_ref, slab, top, bottom):
    o = jnp.concatenate([top, bottom], axis=0).T
    c = slice(slab * LANES, (slab + 1) * LANES)
    o_ref[:, c] = (o * g_ref[:, c].astype(F32)).astype(BF16)


def _local_softmax(st_c, st_l, floor, exact_max):
    m = jnp.max(st_c, axis=0, keepdims=True)
    if floor is not None:
        m = jnp.maximum(m, floor)
    if exact_max:
        m = jnp.maximum(m, jnp.max(st_l, axis=0, keepdims=True))
    pt_c = jnp.exp2(st_c - m)
    pt_l = jnp.exp2(st_l - m)
    denom = jnp.sum(pt_c, axis=0, keepdims=True) + jnp.sum(pt_l, axis=0, keepdims=True)
    if floor is not None:
        denom = denom + jnp.exp2(floor - m)
    return pt_c.astype(BF16), pt_l.astype(BF16), denom


def _local_pv(pt_c, pt_l, denom, vt_c, vt_l):
    pv = _dot(vt_c, pt_c) + _dot(vt_l, pt_l)
    return pv * (1.0 / denom), _overflowed(denom, pv)


def _streamed_then_exact(run):
    @pl.when(run(False))
    def _():
        run(True)


def _mixer_a_kernel(sink_ref, qt_ref, k0_ref, k1_ref, k2_ref, k3_ref, kc_ref, v0_ref, v1_ref, v2_ref, vc_ref,
                    g_ref, o_ref):
    i = pl.program_id(0)
    n_loc = TM + 2 * A_WINDOW
    k_l = jnp.concatenate([k0_ref[...], k1_ref[...], k2_ref[...], k3_ref[...]], axis=0)
    k_c = kc_ref[...]
    vt_l = jnp.concatenate([v0_ref[0][:, TM - A_WINDOW:], v1_ref[0], v2_ref[0][:, :A_WINDOW]], axis=1)
    vt_c = vc_ref[0]
    kl = lax.broadcasted_iota(jnp.int32, (n_loc, TM), 0)
    ql = lax.broadcasted_iota(jnp.int32, (n_loc, TM), 1)
    kpos = i * TM - A_WINDOW + kl
    dist = kl - ql
    ok = (jnp.where(dist >= 0, 1, 0) * jnp.where(dist <= 2 * A_WINDOW, 1, 0) * jnp.where(kpos >= 0, 1, 0)
          * jnp.where(kpos < S, 1, 0) * jnp.where(i < N_LAT_TILES, 1, 0))
    mask = jnp.where(ok > 0, 0.0, NEG)
    mask4 = jnp.concatenate([mask] * 4, axis=1)
    qt = qt_ref[...]

    def run(exact_max):
        heads = [range(4 * kh, 4 * kh + 4) for kh in range(2)]
        ws = [jnp.concatenate([_zero_padded(qt[HEAD * hq:HEAD * (hq + 1)], kh == 0) for hq in heads[kh]], axis=1)
              for kh in range(2)]
        snks = [jnp.concatenate([jnp.full((1, TM), sink_ref[hq] * LOG2E, F32) for hq in heads[kh]], axis=1)
                for kh in range(2)]
        st_cs = [_dot(k_c, w) for w in ws]
        st_ls = [_dot(k_l, w) + mask4 for w in ws]
        sms = [_local_softmax(st_c, st_l, snk, exact_max) for st_c, st_l, snk in zip(st_cs, st_ls, snks)]
        bad = None
        for kh in range(2):
            ot, overflow = _local_pv(*sms[kh], vt_c, vt_l)
            ot = ot[HEAD * kh:HEAD * (kh + 1)]
            for pair in range(2):
                _store_head_pair(o_ref, g_ref, 2 * kh + pair, ot[:, 2 * pair * TM:(2 * pair + 1) * TM],
                                 ot[:, (2 * pair + 1) * TM:(2 * pair + 2) * TM])
            bad = overflow if bad is None else jnp.logical_or(bad, overflow)
        return bad

    _streamed_then_exact(run)


def _mixer_a(qt, proj, vt, sink):
    nt = T // TM
    last_blk = T // A_WINDOW - 1

    def k_spec(off):
        return pl.BlockSpec((A_WINDOW, LANES),
                            lambda i: (jnp.clip(2 * i + off, 0, last_blk), E_KA // LANES))

    def v_spec(off):
        return pl.BlockSpec((1, LANES, KV_SLAB), lambda i: (jnp.clip(i + off, 0, nt - 1), EV_A // LANES, 0))

    return pl.pallas_call(
        _mixer_a_kernel,
        out_shape=jax.ShapeDtypeStruct((T, 512), BF16),
        grid=(nt,),
        in_specs=[pl.BlockSpec(memory_space=pltpu.SMEM),
                  pl.BlockSpec((512, TM), lambda i: (EQ_A // 512, i))]
        + [k_spec(off) for off in (-1, 0, 1, 2)]
        + [pl.BlockSpec((L, LANES), lambda i: (CTX_TILE, E_KA // LANES))]
        + [v_spec(off) for off in (-1, 0, 1)]
        + [pl.BlockSpec((1, LANES, KV_SLAB), lambda i: (CTX_TILE, EV_A // LANES, 0)),
           pl.BlockSpec((TM, 512), lambda i: (i, E_G // 512))],
        out_specs=pl.BlockSpec((TM, 512), lambda i: (i, 0)),
        compiler_params=_params(("parallel",)),
        name="mixer_a",
    )(sink, qt, *([proj] * 5), *([vt] * 4), proj)


def _mixer_d_kernel(qt_ref, k0_ref, k1_ref, k2_ref, kc_ref, v0_ref, v1_ref, v2_ref, vc_ref, bias_ref, g_ref, o_ref):
    k_l = jnp.concatenate([k0_ref[...], k1_ref[...], k2_ref[...]], axis=0)
    k_c = kc_ref[...]
    vt_l = jnp.concatenate([v0_ref[0], v1_ref[0], v2_ref[0]], axis=1)
    vt_c = vc_ref[0]
    qt = qt_ref[...]

    def run(exact_max):
        pairs = range(4)
        rows = [slice(pair * LANES, (pair + 1) * LANES) for pair in pairs]
        ws = [jnp.concatenate([_zero_padded(qt[r][:HEAD], True), _zero_padded(qt[r][HEAD:], False)], axis=1)
              for r in rows]
        st_cs = [_dot(k_c[:, r], w) for r, w in zip(rows, ws)]
        st_ls = [_dot(k_l[:, r], w) + jnp.concatenate([bias_ref[0, 2 * p], bias_ref[0, 2 * p + 1]], axis=1)
                 for p, r, w in zip(pairs, rows, ws)]
        sms = [_local_softmax(st_c, st_l, None, exact_max) for st_c, st_l in zip(st_cs, st_ls)]
        bad = None
        for p, r, sm in zip(pairs, rows, sms):
            ot, overflow = _local_pv(*sm, vt_c[r], vt_l[r])
            _store_head_pair(o_ref, g_ref, p, ot[:HEAD, :TM], ot[HEAD:, TM:])
            bad = overflow if bad is None else jnp.logical_or(bad, overflow)
        return bad

    _streamed_then_exact(run)


def _mixer_d(qt, proj, vt, bias):
    nt = S // TM

    def k_spec(off):
        return pl.BlockSpec((TM, 512), lambda i: (jnp.clip(i + off, 0, nt - 1), O_KD // 512))

    def v_spec(off):
        return pl.BlockSpec((1, 512, KV_SLAB), lambda i: (jnp.clip(i + off, 0, nt - 1), OV_D // 512, 0))

    def variant(i):
        return jnp.where(i == 0, 0, jnp.where(i == nt - 1, 2, 1))

    return pl.pallas_call(
        _mixer_d_kernel,
        out_shape=jax.ShapeDtypeStruct((S, 512), BF16),
        grid=(nt,),
        in_specs=[pl.BlockSpec((512, TM), lambda i: (OQ_D // 512, i))]
        + [k_spec(off) for off in (-1, 0, 1)]
        + [pl.BlockSpec((L, 512), lambda i: (CTX_TILE, O_KD // 512))]
        + [v_spec(off) for off in (-1, 0, 1)]
        + [pl.BlockSpec((1, 512, KV_SLAB), lambda i: (CTX_TILE, OV_D // 512, 0)),
           pl.BlockSpec((1, 8, 3 * TM, TM), lambda i: (variant(i), 0, 0, 0)),
           pl.BlockSpec((TM, 512), lambda i: (i, O_G // 512 + 1))],
        out_specs=pl.BlockSpec((TM, 512), lambda i: (i, 0)),
        compiler_params=_params(("parallel",)),
        name="mixer_d",
    )(qt, *([proj] * 4), *([vt] * 4), bias, proj)


D_ROWS, D_COLS = 8, 16
RPB_R, RPB_C = 2 * D_ROWS - 1, 2 * D_COLS - 1
Q_GRID_ROWS = Q_ROWS
K_GRID_ROWS = 3 * Q_GRID_ROWS


def _bias_kernel(rpb_ref, o_ref):
    base = pl.program_id(0) * (RPB_R * RPB_C)
    kc = lax.broadcasted_iota(jnp.int32, (GRID_W, LANES), 0)
    lane = lax.broadcasted_iota(jnp.int32, (GRID_W, LANES), 1)
    right = lane >= GRID_W
    qc = jnp.where(right, lane - GRID_W, lane)
    cs = jnp.clip(qc - D_COLS // 2, 0, GRID_W - D_COLS)
    col_ok = jnp.where(kc >= cs, 1, 0) * jnp.where(kc < cs + D_COLS, 1, 0)
    dc = jnp.where(col_ok > 0, kc - qc + (D_COLS - 1), -1)
    dc_l = jnp.where(right, -1, dc)
    dc_r = jnp.where(right, dc, -1)
    neg = jnp.full((GRID_W, LANES), NEG, F32)
    pairs = []
    for d in range(RPB_R - 1):
        t2 = neg
        for j in range(RPB_C):
            t2 = jnp.where(dc_l == j, rpb_ref[base + (d + 1) * RPB_C + j] * LOG2E, t2)
            t2 = jnp.where(dc_r == j, rpb_ref[base + d * RPB_C + j] * LOG2E, t2)
        pairs.append(t2)
    row_ok = (lambda ri, kr: kr >= Q_GRID_ROWS,
              lambda ri, kr: 0 <= kr - ri < D_ROWS,
              lambda ri, kr: kr < D_ROWS)
    for var in range(3):
        for kr in range(K_GRID_ROWS):
            for b in range(Q_GRID_ROWS // 2):
                blk = pairs[kr - 2 * b + 2]
                ok_l, ok_r = row_ok[var](2 * b, kr), row_ok[var](2 * b + 1, kr)
                if not ok_l and not ok_r:
                    blk = neg
                elif not ok_l:
                    blk = jnp.where(right, blk, NEG)
                elif not ok_r:
                    blk = jnp.where(right, NEG, blk)
                o_ref[var, 0, kr * GRID_W:(kr + 1) * GRID_W, b * LANES:(b + 1) * LANES] = blk


def _neighbourhood_bias(rpb):
    n_heads = rpb.shape[0]
    return pl.pallas_call(
        _bias_kernel,
        out_shape=jax.ShapeDtypeStruct((3, n_heads, 3 * TM, TM), F32),
        grid=(n_heads,),
        in_specs=[pl.BlockSpec(memory_space=pltpu.SMEM)],
        out_specs=pl.BlockSpec((3, 1, 3 * TM, TM), lambda h: (0, h, 0, 0)),
        compiler_params=_params(("parallel",)),
        name="nbr_bias",
    )(rpb.reshape(-1))


def _final_kernel(x_ref, y1_ref, y2_ref, w_ref, gate_ref, fg_ref, o_ref):
    y = jnp.concatenate([y1_ref[...], y2_ref[...]], axis=1)
    xn = x_ref[...] + gate_ref[0:1, :] * _dot(y, w_ref[...])
    o_ref[...] = _rms(xn) * fg_ref[...]


def _final_proj(x, y1, y2, w, mod, final_g):
    const = lambda i: (0, 0)
    return pl.pallas_call(
        _final_kernel,
        out_shape=jax.ShapeDtypeStruct((S, D), F32),
        grid=(S // OUT_TM,),
        in_specs=[
            pl.BlockSpec((OUT_TM, D), lambda i: (i, 0)),
            pl.BlockSpec((OUT_TM, 512), lambda i: (i, 0)),
            pl.BlockSpec((OUT_TM, 512), lambda i: (i, 0)),
            pl.BlockSpec((D, D), const),
            pl.BlockSpec((2, D), lambda i: (0, 2)),
            pl.BlockSpec((1, D), const),
        ],
        out_specs=pl.BlockSpec((OUT_TM, D), lambda i: (i, 0)),
        compiler_params=_params(("parallel",)),
        name="final_proj",
    )(x, y1, y2, w, mod, final_g)


def _rope_tables():
    quarter = HEAD // 4
    inv = ROPE_THETA ** (-jnp.arange(quarter, dtype=F32) / quarter)

    def parts(n):
        ang = jnp.arange(n).astype(F32)[:, None] * inv[None, :]
        cos, sin, z = jnp.cos(ang), jnp.sin(ang), jnp.zeros((n, quarter), F32)
        return (jnp.concatenate([cos, cos], axis=1), jnp.concatenate([-sin, z], axis=1),
                jnp.concatenate([z, sin], axis=1))

    def lanes(first32, second32):
        return jnp.tile(jnp.concatenate([first32, second32], axis=1), (1, LANES // HEAD))

    n_rows = S // GRID_W
    zr, zc = jnp.zeros((n_rows, 32), F32), jnp.zeros((GRID_W, 32), F32)
    ident = (jnp.ones((Q_ROWS, 32), F32), jnp.zeros((Q_ROWS, 32), F32), jnp.zeros((Q_ROWS, 32), F32))
    rt = jnp.concatenate([
        jnp.concatenate([lanes(p, zr).reshape(N_LAT_TILES, Q_ROWS, LANES) for p in parts(n_rows)], axis=1),
        jnp.concatenate([lanes(p, jnp.zeros((Q_ROWS, 32), F32)) for p in ident], axis=0)[None]], axis=0)
    ct_lat = jnp.stack([lanes(zc, p) for p in parts(GRID_W)])
    ct_ctx = jnp.stack([lanes(zc, jnp.full((GRID_W, 32), v, F32)) for v in (1.0, 0.0, 0.0)])
    return rt, jnp.stack([ct_lat, ct_ctx])


def kernel(x, c, ctx, c_ctx, ev_norm_g, ev_w_ada, ev_b_ada, ev_w_in, ev_a_sink, ev_b_lambda, ev_b_subln_g, ev_w_out, od_norm_g, od_w_ada, od_b_ada, od_w_in, od_c_q_norm_g, od_c_kv_norm_g, od_c_w_qb, od_c_w_kvb, od_d_rpb, od_w_out, final_norm_g):
    assert x.shape == (1, S, D) and ctx.shape == (1, L, D)
    assert ev_w_in.shape[0] == 1 and od_w_in.shape[0] == 1
    x0, xc0 = x[0], ctx[0]
    ct = jnp.stack([c[0], c_ctx], axis=1)
    tables = _rope_tables()
    qk_scale = HEAD ** -0.5

    mod = _ada(ct, ev_w_ada[0], ev_b_ada[0][None])
    w = ev_w_in[0]
    w_e = jnp.concatenate([w[:, :R_KA] * qk_scale, w[:, R_KA:R_QB], w[:, R_QB:R_KB] * qk_scale, w[:, R_KB:]],
                          axis=1).astype(BF16)
    proj, qt, vt = _proj_even(x0, xc0, mod, ev_norm_g[0][None], w_e, tables)
    ya = _mixer_a(qt, proj, vt, ev_a_sink[0])
    diff_extra = (ev_b_lambda[0], ev_b_subln_g[0][None])
    diff = functools.partial(_flash, qt, proj, vt, diff_extra, n_half=2, dq=128, qrow=EQ_B, kcol=E_KB, vrow=EV_B,
                             gcol=E_G + 512, n_heads=4)
    yb = diff(q_row0=0, n_q_rows=S, kv_row0=0, n_kv_rows=T, tq=FLASH_TQ, tk=FLASH_TK, name="diff_attn")
    ybc = diff(q_row0=S, n_q_rows=L, kv_row0=S, n_kv_rows=L, tq=L, tk=L, name="diff_attn_ctx")
    mod_prev = mod

    mod = _ada(ct, od_w_ada[0], od_b_ada[0][None])
    w = od_w_in[0]
    w_o = jnp.concatenate([w[:, 0:384], w[:, 384:448], w[:, 384:448], w[:, 448:960] * qk_scale, w[:, 960:]],
                          axis=1).astype(BF16)
    wqb = od_c_w_qb[0].reshape(256, 4, MLA_DK)
    wqb = jnp.concatenate([wqb[:, :, :128].reshape(256, 512), wqb[:, :, 128:].reshape(256, 256)],
                          axis=1).astype(BF16)
    x1, proj, qt, vt = _proj_odd(x0, xc0, ya, yb, ybc, ev_w_out[0].astype(BF16), mod_prev, mod, od_norm_g[0][None],
                                 w_o, od_c_q_norm_g[0][None], od_c_kv_norm_g[0][None], wqb,
                                 od_c_w_kvb[0].astype(BF16), tables)
    ym = _flash(qt, proj, vt, (), n_half=1, dq=256, qrow=OQ_M, kcol=O_KM, vrow=OV_M, gcol=O_G, n_heads=4,
                q_row0=0, n_q_rows=S, kv_row0=0, n_kv_rows=T, tq=FLASH_TQ, tk=FLASH_TK, name="mla_attn")
    yd = _mixer_d(qt, proj, vt, _neighbourhood_bias(od_d_rpb[0]))
    out = _final_proj(x1, ym, yd, od_w_out[0].astype(BF16), mod, final_norm_g[None])
    return out[None]
```

```python
import functools
import math

import jax
import jax.numpy as jnp
from jax import lax
from jax.experimental import pallas as pl
from jax.experimental.pallas import tpu as pltpu

F32 = jnp.float32
BF16 = jnp.bfloat16

D = 1024
S = 16384
L = 256
T = S + L
GRID_W = 64
EPS = 1e-6
NEG = -1e30
ROPE_THETA = 10000.0
A_WINDOW = 128
LANES = 128
HEAD = 64
MLA_DK = 192
LAM_INIT = 0.8 - 0.6 * math.exp(-0.3 * 0)

LOG2E = math.log2(math.e)
FLASH_TQ = 512
FLASH_TK = 1024

TM = 256
Q_ROWS = TM // GRID_W
OUT_TM = 512
N_LAT_TILES = S // TM
CTX_TILE = S // TM
VMEM_LIMIT = 56 * 1024 * 1024
KV_SLAB = 256

R_QA, R_KA, R_VA, R_QB, R_KB, R_VB, R_G = 0, 512, 640, 768, 1280, 1792, 2304
R_COLS = 3328
E_G, E_KB, E_KA = 0, 1024, 1536
E_COLS = 1664
EQ_A, EQ_B, EQ_ROWS = 0, 512, 1024
EV_B, EV_A, EV_ROWS = 0, 512, 640
O_G, O_KM, O_KD = 0, 1024, 2048
O_COLS = 2560
OQ_M, OQ_D, OQ_ROWS = 0, 1024, 1536
OV_M, OV_D, OV_ROWS = 0, 512, 1024


def _params(sem):
    return pltpu.CompilerParams(dimension_semantics=sem, vmem_limit_bytes=VMEM_LIMIT)


def _dot(a, b):
    return jnp.dot(a, b, preferred_element_type=F32)


def _silu(z):
    return z / (1.0 + jnp.exp(-z))


def _rms(z):
    return z * lax.rsqrt(jnp.mean(z * z, axis=-1, keepdims=True) + EPS)


def _overflowed(*xs):
    chk = jnp.sum(xs[0] * 0.0)
    for x in xs[1:]:
        chk = chk + jnp.sum(x * 0.0)
    return chk != 0.0


def _ada_kernel(ct_ref, w_ref, b_ref, o_ref):
    sc = _silu(ct_ref[...])
    w = w_ref[...]
    r0 = jnp.sum(w * sc[:, 0:1], axis=0, keepdims=True)
    r1 = jnp.sum(w * sc[:, 1:2], axis=0, keepdims=True)
    o_ref[...] = jnp.concatenate([r0, r1], axis=0) + b_ref[...]


def _ada(ct, w, b):
    tn = 768
    return pl.pallas_call(
        _ada_kernel,
        out_shape=jax.ShapeDtypeStruct((2, 3 * D), F32),
        grid=(3 * D // tn,),
        in_specs=[
            pl.BlockSpec((D, 2), lambda n: (0, 0)),
            pl.BlockSpec((D, tn), lambda n: (0, n)),
            pl.BlockSpec((1, tn), lambda n: (0, n)),
        ],
        out_specs=pl.BlockSpec((2, tn), lambda n: (0, n)),
        compiler_params=_params(("parallel",)),
        name="ada_mod",
    )(ct, w, b)


def _is_latent_tile():
    return pl.program_id(0) < N_LAT_TILES


def _modulate(xin, mod_ref, ng_ref):
    mod = mod_ref[...]
    modr = jnp.where(_is_latent_tile(), mod[0:1], mod[1:2])
    shift, scale = modr[:, :D], modr[:, D:2 * D]
    return (_rms(xin) * ng_ref[...] * (1.0 + scale) + shift).astype(BF16)


def _modulated(x_ref, ctx_ref, mod_ref, ng_ref):
    return _modulate(jnp.where(_is_latent_tile(), x_ref[...], ctx_ref[...]), mod_ref, ng_ref)


def _rope_fn(rt_ref, ct_ref):
    def table(k):
        rows = jnp.concatenate([jnp.broadcast_to(rt_ref[0, Q_ROWS * k + j:Q_ROWS * k + j + 1, :], (GRID_W, LANES))
                                for j in range(Q_ROWS)], axis=0)
        return rows + jnp.concatenate([ct_ref[0, k]] * Q_ROWS, axis=0)

    cos, sa, sb = table(0), table(1), table(2)

    def rope(z):
        return z * cos + pltpu.roll(z, LANES - 16, 1) * sa + pltpu.roll(z, 16, 1) * sb

    return rope


def _tr(z):
    return z.T.astype(BF16)


def _slabs(lo, n):
    return [slice(lo + j * LANES, lo + (j + 1) * LANES) for j in range(n // LANES)]


def _proj_even_kernel(x_ref, ctx_ref, mod_ref, ng_ref, w_ref, rt_ref, ct_ref, o_ref, qt_ref, vt_ref):
    h = _modulated(x_ref, ctx_ref, mod_ref, ng_ref)
    res = _dot(h, w_ref[...])
    rope = _rope_fn(rt_ref, ct_ref)
    for src, dst in zip(_slabs(R_QA, 512) + _slabs(R_QB, 512), _slabs(EQ_A, 1024)):
        qt_ref[dst, :] = _tr(rope(res[:, src]) * LOG2E)
    for src, dst in zip(_slabs(R_KB, 512) + _slabs(R_KA, 128), _slabs(E_KB, 640)):
        o_ref[:, dst] = rope(res[:, src]).astype(BF16)
    for src, dst in zip(_slabs(R_VB, 512) + _slabs(R_VA, 128), _slabs(EV_B, 640)):
        vt_ref[0, dst, :] = _tr(res[:, src])
    o_ref[:, E_G:E_G + D] = _silu(res[:, R_G:]).astype(BF16)


def _proj_odd_kernel(x_ref, ctx_ref, ya_ref, yb_ref, ybc_ref, wout_ref, gate_ref, mod_ref, ng_ref, w_ref,
                     qg_ref, kvg_ref, wqb_ref, wkvb_ref, rt_ref, ct_ref, x1_ref, o_ref, qt_ref, vt_ref):
    is_lat = _is_latent_tile()
    y = jnp.concatenate([ya_ref[...], jnp.where(is_lat, yb_ref[...], ybc_ref[...])], axis=1)
    gate = jnp.where(is_lat, gate_ref[0:1, :], gate_ref[1:2, :])
    x1 = jnp.where(is_lat, x_ref[...], ctx_ref[...]) + gate * _dot(y, wout_ref[...])
    x1_ref[...] = x1
    h = _modulate(x1, mod_ref, ng_ref)
    res = _dot(h, w_ref[...])
    rope = _rope_fn(rt_ref, ct_ref)
    qn = (_rms(res[:, 0:256]) * qg_ref[...]).astype(BF16)
    kvn = (_rms(res[:, 256:384]) * kvg_ref[...]).astype(BF16)
    q = _dot(qn, wqb_ref[...]) * (MLA_DK ** -0.5 * LOG2E)
    kv = _dot(kvn, wkvb_ref[...])
    kpe = rope(res[:, 384:512]).astype(BF16)
    pes = (rope(q[:, 512:640]), rope(q[:, 640:768]))
    left = lax.broadcasted_iota(jnp.int32, (TM, LANES), 1) < HEAD
    for hd in range(4):
        pe = pes[hd // 2]
        pem = jnp.where(left if hd % 2 == 0 else jnp.logical_not(left), pe, 0.0)
        qt_ref[OQ_M + 256 * hd:OQ_M + 256 * hd + 128, :] = _tr(q[:, 128 * hd:128 * hd + 128])
        qt_ref[OQ_M + 256 * hd + 128:OQ_M + 256 * hd + 256, :] = _tr(pem)
        c0 = O_KM + 256 * hd
        o_ref[:, c0:c0 + 128] = kv[:, 256 * hd:256 * hd + 128].astype(BF16)
        o_ref[:, c0 + 128:c0 + 256] = kpe
        vt_ref[0, OV_M + 128 * hd:OV_M + 128 * hd + 128, :] = _tr(kv[:, 256 * hd + 128:256 * hd + 256])
    for src, dst in zip(_slabs(512, 512), _slabs(OQ_D, 512)):
        qt_ref[dst, :] = _tr(res[:, src] * LOG2E)
    o_ref[:, O_KD:O_KD + 512] = res[:, 1024:1536].astype(BF16)
    for src, dst in zip(_slabs(1536, 512), _slabs(OV_D, 512)):
        vt_ref[0, dst, :] = _tr(res[:, src])
    o_ref[:, O_G:O_G + D] = _silu(res[:, 2048:3072]).astype(BF16)


def _row_specs(n_w_cols):
    const = lambda i: (0, 0)
    return [
        pl.BlockSpec((TM, D), lambda i: (jnp.minimum(i, N_LAT_TILES - 1), 0)),
        pl.BlockSpec((L, D), const),
        pl.BlockSpec((2, 3 * D), const),
        pl.BlockSpec((1, D), const),
        pl.BlockSpec((D, n_w_cols), const),
    ]


def _table_specs():
    return [pl.BlockSpec((1, 3 * Q_ROWS, LANES), lambda i: (i, 0, 0)),
            pl.BlockSpec((1, 3, GRID_W, LANES), lambda i: (jnp.where(i < N_LAT_TILES, 0, 1), 0, 0, 0))]


def _proj_outs(n_cols, n_qt_rows, n_vt_rows):
    shapes = (jax.ShapeDtypeStruct((T, n_cols), BF16),
              jax.ShapeDtypeStruct((n_qt_rows, T), BF16),
              jax.ShapeDtypeStruct((T // KV_SLAB, n_vt_rows, KV_SLAB), BF16))
    specs = (pl.BlockSpec((TM, n_cols), lambda i: (i, 0)),
             pl.BlockSpec((n_qt_rows, TM), lambda i: (0, i)),
             pl.BlockSpec((1, n_vt_rows, KV_SLAB), lambda i: (i, 0, 0)))
    return shapes, specs


def _proj_even(x, ctx, mod, ng, w, tables):
    shapes, specs = _proj_outs(E_COLS, EQ_ROWS, EV_ROWS)
    return pl.pallas_call(
        _proj_even_kernel,
        out_shape=shapes,
        grid=(T // TM,),
        in_specs=_row_specs(R_COLS) + _table_specs(),
        out_specs=specs,
        compiler_params=_params(("parallel",)),
        name="proj_even",
    )(x, ctx, mod, ng, w, *tables)


def _proj_odd(x, ctx, ya, yb, ybc, w_out, mod_prev, mod, ng, w, qg, kvg, wqb, wkvb, tables):
    const = lambda i: (0, 0)
    lat = lambda i: (jnp.minimum(i, N_LAT_TILES - 1), 0)
    shapes, specs = _proj_outs(O_COLS, OQ_ROWS, OV_ROWS)
    row_specs = _row_specs(3072)
    return pl.pallas_call(
        _proj_odd_kernel,
        out_shape=(jax.ShapeDtypeStruct((T, D), F32),) + shapes,
        grid=(T // TM,),
        in_specs=row_specs[:2] + [
            pl.BlockSpec((TM, 512), lambda i: (i, 0)),
            pl.BlockSpec((TM, 512), lat),
            pl.BlockSpec((L, 512), const),
            pl.BlockSpec((D, D), const),
            pl.BlockSpec((2, D), lambda i: (0, 2)),
        ] + row_specs[2:] + [
            pl.BlockSpec((1, 256), const),
            pl.BlockSpec((1, 128), const),
            pl.BlockSpec((256, 768), const),
            pl.BlockSpec((128, 1024), const),
        ] + _table_specs(),
        out_specs=(pl.BlockSpec((TM, D), lambda i: (i, 0)),) + specs,
        compiler_params=_params(("parallel",)),
        name="proj_odd",
    )(x, ctx, ya, yb, ybc, w_out, mod_prev, mod, ng, w, qg, kvg, wqb, wkvb, *tables)


def _flash_kernel(*refs, n_half, n_chunks, tk, tail):
    if n_half == 2:
        qt_ref, k_ref, vt_ref, g_ref, bl_ref, sg_ref, o_ref = refs
    else:
        qt_ref, k_ref, vt_ref, g_ref, o_ref = refs
    qt = qt_ref[...]
    tq = qt.shape[1]
    if n_half == 2:
        z = jnp.zeros((HEAD, tq), BF16)
        qs = (jnp.concatenate([qt[:HEAD], z], axis=0), jnp.concatenate([z, qt[HEAD:]], axis=0))
    else:
        qs = (qt,)

    def scores(r0, n):
        k = k_ref[pl.ds(r0, n), :]
        s0 = r0 // KV_SLAB
        vt = jnp.concatenate([vt_ref[s0 + j] for j in range(n // KV_SLAB)], axis=1)
        return [_dot(k, q) for q in qs], vt

    def accumulate(sts, vt, carry, exact_max):
        new = []
        for t in range(n_half):
            m_prev, l_prev, acc = carry[t]
            st = sts[t]
            if exact_max:
                m_new = jnp.maximum(m_prev, jnp.max(st, axis=0, keepdims=True))
                alpha = jnp.exp2(m_prev - m_new)
                pt = jnp.exp2(st - m_new)
                l_new = alpha * l_prev + jnp.sum(pt, axis=0, keepdims=True)
                acc = alpha * acc + _dot(vt, pt.astype(BF16))
            else:
                m_new = m_prev
                pt = jnp.exp2(st - m_prev)
                l_new = l_prev + jnp.sum(pt, axis=0, keepdims=True)
                acc = acc + _dot(vt, pt.astype(BF16))
            new.append((m_new, l_new, acc))
        return tuple(new)

    carry = tuple((jnp.full((1, tq), NEG, F32), jnp.zeros((1, tq), F32), jnp.zeros((LANES, tq), F32))
                  for _ in range(n_half))
    if tail:
        carry = accumulate(*scores(n_chunks * tk, tail), carry, True)
    if n_chunks == 1:
        carry = accumulate(*scores(0, tk), carry, not tail)
    elif n_chunks > 1:
        seeded = carry
        ahead = scores(0, tk)
        for c in range(n_chunks):
            cur, ahead = ahead, (scores((c + 1) * tk, tk) if c + 1 < n_chunks else None)
            carry = accumulate(*cur, carry, False)

        def redo(_):
            return lax.fori_loop(
                0, n_chunks, lambda c, cr: accumulate(*scores(pl.multiple_of(c * tk, tk), tk), cr, True), seeded)

        carry = lax.cond(_overflowed(*[x for cr in carry for x in cr[1:]]), redo, lambda cr: cr, carry)
    if n_half == 2:
        bl = bl_ref[...]
        lam = (jnp.exp(jnp.sum(bl[0:1] * bl[1:2], axis=1, keepdims=True))
               - jnp.exp(jnp.sum(bl[2:3] * bl[3:4], axis=1, keepdims=True)) + LAM_INIT)
        ot = carry[0][2] / carry[0][1] - lam * (carry[1][2] / carry[1][1])
        o = _rms(ot.T) * sg_ref[...] * (1.0 - LAM_INIT)
    else:
        o = (carry[0][2] / carry[0][1]).T
    o_ref[...] = (o * g_ref[...].astype(F32)).astype(BF16)


def _flash(qt, proj, vt, extra, *, n_half, dq, qrow, kcol, vrow, gcol, n_heads, q_row0, n_q_rows, kv_row0,
           n_kv_rows, tq, tk, name):
    nq, n_chunks = n_q_rows // tq, n_kv_rows // tk
    tail = n_kv_rows - n_chunks * tk
    qb0, kb0 = q_row0 // tq, kv_row0 // n_kv_rows
    in_specs = [
        pl.BlockSpec((dq, tq), lambda h, i: (qrow // dq + h, qb0 + i)),
        pl.BlockSpec((n_kv_rows, dq), lambda h, i: (kb0, kcol // dq + h)),
        pl.BlockSpec((n_kv_rows // KV_SLAB, LANES, KV_SLAB), lambda h, i: (kb0, vrow // LANES + h, 0)),
        pl.BlockSpec((tq, LANES), lambda h, i: (qb0 + i, gcol // LANES + h)),
    ]
    args = [qt, proj, vt, proj]
    if n_half == 2:
        in_specs += [pl.BlockSpec((4, HEAD), lambda h, i: (0, 0)),
                     pl.BlockSpec((1, LANES), lambda h, i: (0, 0))]
        args += list(extra)
    return pl.pallas_call(
        functools.partial(_flash_kernel, n_half=n_half, n_chunks=n_chunks, tk=tk, tail=tail),
        out_shape=jax.ShapeDtypeStruct((n_q_rows, n_heads * LANES), BF16),
        grid=(n_heads, nq),
        in_specs=in_specs,
        out_specs=pl.BlockSpec((tq, LANES), lambda h, i: (i, h)),
        compiler_params=_params(("parallel", "parallel")),
        name=name,
    )(*args)


def _zero_padded(qh, upper):
    z = jnp.zeros_like(qh)
    return jnp.concatenate([qh, z] if upper else [z, qh], axis=0)


def _store_head_pair(o_ref, g_ref, slab, top, bottom):
    o = jnp.concatenate([top, bottom], axis=0).T
    c = slice(slab * LANES, (slab + 1) * LANES)
    o_ref[:, c] = (o * g_ref[:, c].astype(F32)).astype(BF16)


def _local_softmax(st_c, st_l, floor, exact_max):
    m = jnp.max(st_c, axis=0, keepdims=True)
    if floor is not None:
        m = jnp.maximum(m, floor)
    if exact_max:
        m = jnp.maximum(m, jnp.max(st_l, axis=0, keepdims=True))
    pt_c = jnp.exp2(st_c - m)
    pt_l = jnp.exp2(st_l - m)
    denom = jnp.sum(pt_c, axis=0, keepdims=True) + jnp.sum(pt_l, axis=0, keepdims=True)
    if floor is not None:
        denom = denom + jnp.exp2(floor - m)
    return pt_c.astype(BF16), pt_l.astype(BF16), denom


def _local_pv(pt_c, pt_l, denom, vt_c, vt_l):
    pv = _dot(vt_c, pt_c) + _dot(vt_l, pt_l)
    return pv * (1.0 / denom), _overflowed(denom, pv)


def _streamed_then_exact(run):
    @pl.when(run(False))
    def _():
        run(True)


def _mixer_a_kernel(sink_ref, qt_ref, k0_ref, k1_ref, k2_ref, k3_ref, kc_ref, v0_ref, v1_ref, v2_ref, vc_ref,
                    g_ref, o_ref):
    i = pl.program_id(0)
    n_loc = TM + 2 * A_WINDOW
    k_l = jnp.concatenate([k0_ref[...], k1_ref[...], k2_ref[...], k3_ref[...]], axis=0)
    k_c = kc_ref[...]
    vt_l = jnp.concatenate([v0_ref[0][:, TM - A_WINDOW:], v1_ref[0], v2_ref[0][:, :A_WINDOW]], axis=1)
    vt_c = vc_ref[0]
    kl = lax.broadcasted_iota(jnp.int32, (n_loc, TM), 0)
    ql = lax.broadcasted_iota(jnp.int32, (n_loc, TM), 1)
    kpos = i * TM - A_WINDOW + kl
    dist = kl - ql
    ok = (jnp.where(dist >= 0, 1, 0) * jnp.where(dist <= 2 * A_WINDOW, 1, 0) * jnp.where(kpos >= 0, 1, 0)
          * jnp.where(kpos < S, 1, 0) * jnp.where(i < N_LAT_TILES, 1, 0))
    mask = jnp.where(ok > 0, 0.0, NEG)
    mask2 = jnp.concatenate([mask] * 2, axis=1)
    qt = qt_ref[...]

    def run(exact_max):
        pairs = range(4)
        ws = [jnp.concatenate([_zero_padded(qt[HEAD * hq:HEAD * (hq + 1)], p < 2) for hq in (2 * p, 2 * p + 1)],
                              axis=1) for p in pairs]
        snks = [jnp.concatenate([jnp.full((1, TM), sink_ref[hq] * LOG2E, F32) for hq in (2 * p, 2 * p + 1)],
                                axis=1) for p in pairs]
        st_cs = [_dot(k_c, w) for w in ws]
        st_ls = [_dot(k_l, w) + mask2 for w in ws]
        sms = [_local_softmax(st_c, st_l, snk, exact_max) for st_c, st_l, snk in zip(st_cs, st_ls, snks)]
        bad = None
        for p in pairs:
            ot, overflow = _local_pv(*sms[p], vt_c, vt_l)
            ot = ot[HEAD * (p // 2):HEAD * (p // 2 + 1)]
            _store_head_pair(o_ref, g_ref, p, ot[:, :TM], ot[:, TM:])
            bad = overflow if bad is None else jnp.logical_or(bad, overflow)
        return bad

    _streamed_then_exact(run)


def _mixer_a(qt, proj, vt, sink):
    nt = T // TM
    last_blk = T // A_WINDOW - 1

    def k_spec(off):
        return pl.BlockSpec((A_WINDOW, LANES),
                            lambda i: (jnp.clip(2 * i + off, 0, last_blk), E_KA // LANES))

    def v_spec(off):
        return pl.BlockSpec((1, LANES, KV_SLAB), lambda i: (jnp.clip(i + off, 0, nt - 1), EV_A // LANES, 0))

    return pl.pallas_call(
        _mixer_a_kernel,
        out_shape=jax.ShapeDtypeStruct((T, 512), BF16),
        grid=(nt,),
        in_specs=[pl.BlockSpec(memory_space=pltpu.SMEM),
                  pl.BlockSpec((512, TM), lambda i: (EQ_A // 512, i))]
        + [k_spec(off) for off in (-1, 0, 1, 2)]
        + [pl.BlockSpec((L, LANES), lambda i: (CTX_TILE, E_KA // LANES))]
        + [v_spec(off) for off in (-1, 0, 1)]
        + [pl.BlockSpec((1, LANES, KV_SLAB), lambda i: (CTX_TILE, EV_A // LANES, 0)),
           pl.BlockSpec((TM, 512), lambda i: (i, E_G // 512))],
        out_specs=pl.BlockSpec((TM, 512), lambda i: (i, 0)),
        compiler_params=_params(("parallel",)),
        name="mixer_a",
    )(sink, qt, *([proj] * 5), *([vt] * 4), proj)


def _mixer_d_kernel(qt_ref, k0_ref, k1_ref, k2_ref, kc_ref, v0_ref, v1_ref, v2_ref, vc_ref, bias_ref, g_ref, o_ref):
    k_l = jnp.concatenate([k0_ref[...], k1_ref[...], k2_ref[...]], axis=0)
    k_c = kc_ref[...]
    vt_l = jnp.concatenate([v0_ref[0], v1_ref[0], v2_ref[0]], axis=1)
    vt_c = vc_ref[0]
    qt = qt_ref[...]

    def run(exact_max):
        pairs = range(4)
        rows = [slice(pair * LANES, (pair + 1) * LANES) for pair in pairs]
        ws = [jnp.concatenate([_zero_padded(qt[r][:HEAD], True), _zero_padded(qt[r][HEAD:], False)], axis=1)
              for r in rows]
        st_cs = [_dot(k_c[:, r], w) for r, w in zip(rows, ws)]
        st_ls = [_dot(k_l[:, r], w) + jnp.concatenate([bias_ref[0, 2 * p], bias_ref[0, 2 * p + 1]], axis=1)
                 for p, r, w in zip(pairs, rows, ws)]
        sms = [_local_softmax(st_c, st_l, None, exact_max) for st_c, st_l in zip(st_cs, st_ls)]
        bad = None
        for p, r, sm in zip(pairs, rows, sms):
            ot, overflow = _local_pv(*sm, vt_c[r], vt_l[r])
            _store_head_pair(o_ref, g_ref, p, ot[:HEAD, :TM], ot[HEAD:, TM:])
            bad = overflow if bad is None else jnp.logical_or(bad, overflow)
        return bad

    _streamed_then_exact(run)


def _mixer_d(qt, proj, vt, bias):
    nt = S // TM

    def k_spec(off):
        return pl.BlockSpec((TM, 512), lambda i: (jnp.clip(i + off, 0, nt - 1), O_KD // 512))

    def v_spec(off):
        return pl.BlockSpec((1, 512, KV_SLAB), lambda i: (jnp.clip(i + off, 0, nt - 1), OV_D // 512, 0))

    def variant(i):
        return jnp.where(i == 0, 0, jnp.where(i == nt - 1, 2, 1))

    return pl.pallas_call(
        _mixer_d_kernel,
        out_shape=jax.ShapeDtypeStruct((S, 512), BF16),
        grid=(nt,),
        in_specs=[pl.BlockSpec((512, TM), lambda i: (OQ_D // 512, i))]
        + [k_spec(off) for off in (-1, 0, 1)]
        + [pl.BlockSpec((L, 512), lambda i: (CTX_TILE, O_KD // 512))]
        + [v_spec(off) for off in (-1, 0, 1)]
        + [pl.BlockSpec((1, 512, KV_SLAB), lambda i: (CTX_TILE, OV_D // 512, 0)),
           pl.BlockSpec((1, 8, 3 * TM, TM), lambda i: (variant(i), 0, 0, 0)),
           pl.BlockSpec((TM, 512), lambda i: (i, O_G // 512 + 1))],
        out_specs=pl.BlockSpec((TM, 512), lambda i: (i, 0)),
        compiler_params=_params(("parallel",)),
        name="mixer_d",
    )(qt, *([proj] * 4), *([vt] * 4), bias, proj)


D_ROWS, D_COLS = 8, 16
RPB_R, RPB_C = 2 * D_ROWS - 1, 2 * D_COLS - 1
Q_GRID_ROWS = Q_ROWS
K_GRID_ROWS = 3 * Q_GRID_ROWS


def _bias_kernel(rpb_ref, o_ref):
    base = pl.program_id(0) * (RPB_R * RPB_C)
    kc = lax.broadcasted_iota(jnp.int32, (GRID_W, LANES), 0)
    lane = lax.broadcasted_iota(jnp.int32, (GRID_W, LANES), 1)
    right = lane >= GRID_W
    qc = jnp.where(right, lane - GRID_W, lane)
    cs = jnp.clip(qc - D_COLS // 2, 0, GRID_W - D_COLS)
    col_ok = jnp.where(kc >= cs, 1, 0) * jnp.where(kc < cs + D_COLS, 1, 0)
    dc = jnp.where(col_ok > 0, kc - qc + (D_COLS - 1), -1)
    dc_l = jnp.where(right, -1, dc)
    dc_r = jnp.where(right, dc, -1)
    neg = jnp.full((GRID_W, LANES), NEG, F32)
    pairs = []
    for d in range(RPB_R - 1):
        t2 = neg
        for j in range(RPB_C):
            t2 = jnp.where(dc_l == j, rpb_ref[base + (d + 1) * RPB_C + j] * LOG2E, t2)
            t2 = jnp.where(dc_r == j, rpb_ref[base + d * RPB_C + j] * LOG2E, t2)
        pairs.append(t2)
    row_ok = (lambda ri, kr: kr >= Q_GRID_ROWS,
              lambda ri, kr: 0 <= kr - ri < D_ROWS,
              lambda ri, kr: kr < D_ROWS)
    for var in range(3):
        for kr in range(K_GRID_ROWS):
            for b in range(Q_GRID_ROWS // 2):
                blk = pairs[kr - 2 * b + 2]
                ok_l, ok_r = row_ok[var](2 * b, kr), row_ok[var](2 * b + 1, kr)
                if not ok_l and not ok_r:
                    blk = neg
                elif not ok_l:
                    blk = jnp.where(right, blk, NEG)
                elif not ok_r:
                    blk = jnp.where(right, NEG, blk)
                o_ref[var, 0, kr * GRID_W:(kr + 1) * GRID_W, b * LANES:(b + 1) * LANES] = blk


def _neighbourhood_bias(rpb):
    n_heads = rpb.shape[0]
    return pl.pallas_call(
        _bias_kernel,
        out_shape=jax.ShapeDtypeStruct((3, n_heads, 3 * TM, TM), F32),
        grid=(n_heads,),
        in_specs=[pl.BlockSpec(memory_space=pltpu.SMEM)],
        out_specs=pl.BlockSpec((3, 1, 3 * TM, TM), lambda h: (0, h, 0, 0)),
        compiler_params=_params(("parallel",)),
        name="nbr_bias",
    )(rpb.reshape(-1))


def _final_kernel(x_ref, y1_ref, y2_ref, w_ref, gate_ref, fg_ref, o_ref):
    y = jnp.concatenate([y1_ref[...], y2_ref[...]], axis=1)
    xn = x_ref[...] + gate_ref[0:1, :] * _dot(y, w_ref[...])
    o_ref[...] = _rms(xn) * fg_ref[...]


def _final_proj(x, y1, y2, w, mod, final_g):
    const = lambda i: (0, 0)
    return pl.pallas_call(
        _final_kernel,
        out_shape=jax.ShapeDtypeStruct((S, D), F32),
        grid=(S // OUT_TM,),
        in_specs=[
            pl.BlockSpec((OUT_TM, D), lambda i: (i, 0)),
            pl.BlockSpec((OUT_TM, 512), lambda i: (i, 0)),
            pl.BlockSpec((OUT_TM, 512), lambda i: (i, 0)),
            pl.BlockSpec((D, D), const),
            pl.BlockSpec((2, D), lambda i: (0, 2)),
            pl.BlockSpec((1, D), const),
        ],
        out_specs=pl.BlockSpec((OUT_TM, D), lambda i: (i, 0)),
        compiler_params=_params(("parallel",)),
        name="final_proj",
    )(x, y1, y2, w, mod, final_g)


def _rope_tables():
    quarter = HEAD // 4
    inv = ROPE_THETA ** (-jnp.arange(quarter, dtype=F32) / quarter)

    def parts(n):
        ang = jnp.arange(n).astype(F32)[:, None] * inv[None, :]
        cos, sin, z = jnp.cos(ang), jnp.sin(ang), jnp.zeros((n, quarter), F32)
        return (jnp.concatenate([cos, cos], axis=1), jnp.concatenate([-sin, z], axis=1),
                jnp.concatenate([z, sin], axis=1))

    def lanes(first32, second32):
        return jnp.tile(jnp.concatenate([first32, second32], axis=1), (1, LANES // HEAD))

    n_rows = S // GRID_W
    zr, zc = jnp.zeros((n_rows, 32), F32), jnp.zeros((GRID_W, 32), F32)
    ident = (jnp.ones((Q_ROWS, 32), F32), jnp.zeros((Q_ROWS, 32), F32), jnp.zeros((Q_ROWS, 32), F32))
    rt = jnp.concatenate([
        jnp.concatenate([lanes(p, zr).reshape(N_LAT_TILES, Q_ROWS, LANES) for p in parts(n_rows)], axis=1),
        jnp.concatenate([lanes(p, jnp.zeros((Q_ROWS, 32), F32)) for p in ident], axis=0)[None]], axis=0)
    ct_lat = jnp.stack([lanes(zc, p) for p in parts(GRID_W)])
    ct_ctx = jnp.stack([lanes(zc, jnp.full((GRID_W, 32), v, F32)) for v in (1.0, 0.0, 0.0)])
    return rt, jnp.stack([ct_lat, ct_ctx])


def kernel(x, c, ctx, c_ctx, ev_norm_g, ev_w_ada, ev_b_ada, ev_w_in, ev_a_sink, ev_b_lambda, ev_b_subln_g, ev_w_out, od_norm_g, od_w_ada, od_b_ada, od_w_in, od_c_q_norm_g, od_c_kv_norm_g, od_c_w_qb, od_c_w_kvb, od_d_rpb, od_w_out, final_norm_g):
    assert x.shape == (1, S, D) and ctx.shape == (1, L, D)
    assert ev_w_in.shape[0] == 1 and od_w_in.shape[0] == 1
    x0, xc0 = x[0], ctx[0]
    ct = jnp.stack([c[0], c_ctx], axis=1)
    tables = _rope_tables()
    qk_scale = HEAD ** -0.5

    mod = _ada(ct, ev_w_ada[0], ev_b_ada[0][None])
    w = ev_w_in[0]
    w_e = jnp.concatenate([w[:, :R_KA] * qk_scale, w[:, R_KA:R_QB], w[:, R_QB:R_KB] * qk_scale, w[:, R_KB:]],
                          axis=1).astype(BF16)
    proj, qt, vt = _proj_even(x0, xc0, mod, ev_norm_g[0][None], w_e, tables)
    ya = _mixer_a(qt, proj, vt, ev_a_sink[0])
    diff_extra = (ev_b_lambda[0], ev_b_subln_g[0][None])
    diff = functools.partial(_flash, qt, proj, vt, diff_extra, n_half=2, dq=128, qrow=EQ_B, kcol=E_KB, vrow=EV_B,
                             gcol=E_G + 512, n_heads=4)
    yb = diff(q_row0=0, n_q_rows=S, kv_row0=0, n_kv_rows=T, tq=FLASH_TQ, tk=FLASH_TK, name="diff_attn")
    ybc = diff(q_row0=S, n_q_rows=L, kv_row0=S, n_kv_rows=L, tq=L, tk=L, name="diff_attn_ctx")
    mod_prev = mod

    mod = _ada(ct, od_w_ada[0], od_b_ada[0][None])
    w = od_w_in[0]
    w_o = jnp.concatenate([w[:, 0:384], w[:, 384:448], w[:, 384:448], w[:, 448:960] * qk_scale, w[:, 960:]],
                          axis=1).astype(BF16)
    wqb = od_c_w_qb[0].reshape(256, 4, MLA_DK)
    wqb = jnp.concatenate([wqb[:, :, :128].reshape(256, 512), wqb[:, :, 128:].reshape(256, 256)],
                          axis=1).astype(BF16)
    x1, proj, qt, vt = _proj_odd(x0, xc0, ya, yb, ybc, ev_w_out[0].astype(BF16), mod_prev, mod, od_norm_g[0][None],
                                 w_o, od_c_q_norm_g[0][None], od_c_kv_norm_g[0][None], wqb,
                                 od_c_w_kvb[0].astype(BF16), tables)
    ym = _flash(qt, proj, vt, (), n_half=1, dq=256, qrow=OQ_M, kcol=O_KM, vrow=OV_M, gcol=O_G, n_heads=4,
                q_row0=0, n_q_rows=S, kv_row0=0, n_kv_rows=T, tq=FLASH_TQ, tk=FLASH_TK, name="mla_attn")
    yd = _mixer_d(qt, proj, vt, _neighbourhood_bias(od_d_rpb[0]))
    out = _final_proj(x1, ym, yd, od_w_out[0].astype(BF16), mod, final_norm_g[None])
    return out[None]
```

```python
import functools
import math

import jax
import jax.numpy as jnp
from jax import lax
from jax.experimental import pallas as pl
from jax.experimental.pallas import tpu as pltpu

F32 = jnp.float32
BF16 = jnp.bfloat16

D = 1024
S = 16384
L = 256
T = S + L
GRID_W = 64
EPS = 1e-6
NEG = -1e30
ROPE_THETA = 10000.0
A_WINDOW = 128
LANES = 128
HEAD = 64
MLA_DK = 192
LAM_INIT = 0.8 - 0.6 * math.exp(-0.3 * 0)

LOG2E = math.log2(math.e)
FLASH_TQ = 512
FLASH_TK = 1024

TM = 256
Q_ROWS = TM // GRID_W
OUT_TM = 1024
N_LAT_TILES = S // TM
CTX_TILE = S // TM
VMEM_LIMIT = 56 * 1024 * 1024
KV_SLAB = 256

R_QA, R_KA, R_VA, R_QB, R_KB, R_VB, R_G = 0, 512, 640, 768, 1280, 1792, 2304
R_COLS = 3328
E_G, E_KB, E_KA = 0, 1024, 1536
E_COLS = 1664
EQ_A, EQ_B, EQ_ROWS = 0, 512, 1024
EV_B, EV_A, EV_ROWS = 0, 512, 640
O_G, O_KM, O_KD = 0, 1024, 2048
O_COLS = 2560
OQ_M, OQ_D, OQ_ROWS = 0, 1024, 1536
OV_M, OV_D, OV_ROWS = 0, 512, 1024


def _params(sem):
    return pltpu.CompilerParams(dimension_semantics=sem, vmem_limit_bytes=VMEM_LIMIT)


def _dot(a, b):
    return jnp.dot(a, b, preferred_element_type=F32)


def _silu(z):
    return z / (1.0 + jnp.exp(-z))


def _rms(z):
    return z * lax.rsqrt(jnp.mean(z * z, axis=-1, keepdims=True) + EPS)


def _overflowed(*xs):
    chk = jnp.sum(xs[0] * 0.0)
    for x in xs[1:]:
        chk = chk + jnp.sum(x * 0.0)
    return chk != 0.0


def _ada_kernel(ct_ref, w_ref, b_ref, o_ref):
    sc = _silu(ct_ref[...])
    w = w_ref[...]
    r0 = jnp.sum(w * sc[:, 0:1], axis=0, keepdims=True)
    r1 = jnp.sum(w * sc[:, 1:2], axis=0, keepdims=True)
    o_ref[...] = jnp.concatenate([r0, r1], axis=0) + b_ref[...]


def _ada(ct, w, b):
    tn = 768
    return pl.pallas_call(
        _ada_kernel,
        out_shape=jax.ShapeDtypeStruct((2, 3 * D), F32),
        grid=(3 * D // tn,),
        in_specs=[
            pl.BlockSpec((D, 2), lambda n: (0, 0)),
            pl.BlockSpec((D, tn), lambda n: (0, n)),
            pl.BlockSpec((1, tn), lambda n: (0, n)),
        ],
        out_specs=pl.BlockSpec((2, tn), lambda n: (0, n)),
        compiler_params=_params(("parallel",)),
        name="ada_mod",
    )(ct, w, b)


def _is_latent_tile():
    return pl.program_id(0) < N_LAT_TILES


def _modulate(xin, mod_ref, ng_ref):
    mod = mod_ref[...]
    modr = jnp.where(_is_latent_tile(), mod[0:1], mod[1:2])
    shift, scale = modr[:, :D], modr[:, D:2 * D]
    return (_rms(xin) * ng_ref[...] * (1.0 + scale) + shift).astype(BF16)


def _modulated(x_ref, ctx_ref, mod_ref, ng_ref):
    return _modulate(jnp.where(_is_latent_tile(), x_ref[...], ctx_ref[...]), mod_ref, ng_ref)


def _rope_fn(rt_ref, ct_ref):
    def table(k):
        rows = jnp.concatenate([jnp.broadcast_to(rt_ref[0, Q_ROWS * k + j:Q_ROWS * k + j + 1, :], (GRID_W, LANES))
                                for j in range(Q_ROWS)], axis=0)
        return rows + jnp.concatenate([ct_ref[0, k]] * Q_ROWS, axis=0)

    cos, sa, sb = table(0), table(1), table(2)

    def rope(z):
        return z * cos + pltpu.roll(z, LANES - 16, 1) * sa + pltpu.roll(z, 16, 1) * sb

    return rope


def _tr(z):
    return z.T.astype(BF16)


def _slabs(lo, n):
    return [slice(lo + j * LANES, lo + (j + 1) * LANES) for j in range(n // LANES)]


def _proj_even_kernel(x_ref, ctx_ref, mod_ref, ng_ref, w_ref, rt_ref, ct_ref, o_ref, qt_ref, vt_ref):
    h = _modulated(x_ref, ctx_ref, mod_ref, ng_ref)
    res = _dot(h, w_ref[...])
    rope = _rope_fn(rt_ref, ct_ref)
    for src, dst in zip(_slabs(R_QA, 512) + _slabs(R_QB, 512), _slabs(EQ_A, 1024)):
        qt_ref[dst, :] = _tr(rope(res[:, src]) * LOG2E)
    for src, dst in zip(_slabs(R_KB, 512) + _slabs(R_KA, 128), _slabs(E_KB, 640)):
        o_ref[:, dst] = rope(res[:, src]).astype(BF16)
    for src, dst in zip(_slabs(R_VB, 512) + _slabs(R_VA, 128), _slabs(EV_B, 640)):
        vt_ref[0, dst, :] = _tr(res[:, src])
    o_ref[:, E_G:E_G + D] = _silu(res[:, R_G:]).astype(BF16)


def _proj_odd_kernel(x_ref, ctx_ref, ya_ref, yb_ref, ybc_ref, wout_ref, gate_ref, mod_ref, ng_ref, w_ref,
                     qg_ref, kvg_ref, wqb_ref, wkvb_ref, rt_ref, ct_ref, x1_ref, o_ref, qt_ref, vt_ref):
    is_lat = _is_latent_tile()
    y = jnp.concatenate([ya_ref[...], jnp.where(is_lat, yb_ref[...], ybc_ref[...])], axis=1)
    gate = jnp.where(is_lat, gate_ref[0:1, :], gate_ref[1:2, :])
    x1 = jnp.where(is_lat, x_ref[...], ctx_ref[...]) + gate * _dot(y, wout_ref[...])
    x1_ref[...] = x1
    h = _modulate(x1, mod_ref, ng_ref)
    res = _dot(h, w_ref[...])
    rope = _rope_fn(rt_ref, ct_ref)
    qn = (_rms(res[:, 0:256]) * qg_ref[...]).astype(BF16)
    kvn = (_rms(res[:, 256:384]) * kvg_ref[...]).astype(BF16)
    q = _dot(qn, wqb_ref[...]) * (MLA_DK ** -0.5 * LOG2E)
    kv = _dot(kvn, wkvb_ref[...])
    kpe = rope(res[:, 384:512]).astype(BF16)
    pes = (rope(q[:, 512:640]), rope(q[:, 640:768]))
    left = lax.broadcasted_iota(jnp.int32, (TM, LANES), 1) < HEAD
    for hd in range(4):
        pe = pes[hd // 2]
        pem = jnp.where(left if hd % 2 == 0 else jnp.logical_not(left), pe, 0.0)
        qt_ref[OQ_M + 256 * hd:OQ_M + 256 * hd + 128, :] = _tr(q[:, 128 * hd:128 * hd + 128])
        qt_ref[OQ_M + 256 * hd + 128:OQ_M + 256 * hd + 256, :] = _tr(pem)
        c0 = O_KM + 256 * hd
        o_ref[:, c0:c0 + 128] = kv[:, 256 * hd:256 * hd + 128].astype(BF16)
        o_ref[:, c0 + 128:c0 + 256] = kpe
        vt_ref[0, OV_M + 128 * hd:OV_M + 128 * hd + 128, :] = _tr(kv[:, 256 * hd + 128:256 * hd + 256])
    for src, dst in zip(_slabs(512, 512), _slabs(OQ_D, 512)):
        qt_ref[dst, :] = _tr(res[:, src] * LOG2E)
    o_ref[:, O_KD:O_KD + 512] = res[:, 1024:1536].astype(BF16)
    for src, dst in zip(_slabs(1536, 512), _slabs(OV_D, 512)):
        vt_ref[0, dst, :] = _tr(res[:, src])
    o_ref[:, O_G:O_G + D] = _silu(res[:, 2048:3072]).astype(BF16)


def _row_specs(n_w_cols):
    const = lambda i: (0, 0)
    return [
        pl.BlockSpec((TM, D), lambda i: (jnp.minimum(i, N_LAT_TILES - 1), 0)),
        pl.BlockSpec((L, D), const),
        pl.BlockSpec((2, 3 * D), const),
        pl.BlockSpec((1, D), const),
        pl.BlockSpec((D, n_w_cols), const),
    ]


def _table_specs():
    return [pl.BlockSpec((1, 3 * Q_ROWS, LANES), lambda i: (i, 0, 0)),
            pl.BlockSpec((1, 3, GRID_W, LANES), lambda i: (jnp.where(i < N_LAT_TILES, 0, 1), 0, 0, 0))]


def _proj_outs(n_cols, n_qt_rows, n_vt_rows):
    shapes = (jax.ShapeDtypeStruct((T, n_cols), BF16),
              jax.ShapeDtypeStruct((n_qt_rows, T), BF16),
              jax.ShapeDtypeStruct((T // KV_SLAB, n_vt_rows, KV_SLAB), BF16))
    specs = (pl.BlockSpec((TM, n_cols), lambda i: (i, 0)),
             pl.BlockSpec((n_qt_rows, TM), lambda i: (0, i)),
             pl.BlockSpec((1, n_vt_rows, KV_SLAB), lambda i: (i, 0, 0)))
    return shapes, specs


def _proj_even(x, ctx, mod, ng, w, tables):
    shapes, specs = _proj_outs(E_COLS, EQ_ROWS, EV_ROWS)
    return pl.pallas_call(
        _proj_even_kernel,
        out_shape=shapes,
        grid=(T // TM,),
        in_specs=_row_specs(R_COLS) + _table_specs(),
        out_specs=specs,
        compiler_params=_params(("parallel",)),
        name="proj_even",
    )(x, ctx, mod, ng, w, *tables)


def _proj_odd(x, ctx, ya, yb, ybc, w_out, mod_prev, mod, ng, w, qg, kvg, wqb, wkvb, tables):
    const = lambda i: (0, 0)
    lat = lambda i: (jnp.minimum(i, N_LAT_TILES - 1), 0)
    shapes, specs = _proj_outs(O_COLS, OQ_ROWS, OV_ROWS)
    row_specs = _row_specs(3072)
    return pl.pallas_call(
        _proj_odd_kernel,
        out_shape=(jax.ShapeDtypeStruct((T, D), F32),) + shapes,
        grid=(T // TM,),
        in_specs=row_specs[:2] + [
            pl.BlockSpec((TM, 512), lambda i: (i, 0)),
            pl.BlockSpec((TM, 512), lat),
            pl.BlockSpec((L, 512), const),
            pl.BlockSpec((D, D), const),
            pl.BlockSpec((2, D), lambda i: (0, 2)),
        ] + row_specs[2:] + [
            pl.BlockSpec((1, 256), const),
            pl.BlockSpec((1, 128), const),
            pl.BlockSpec((256, 768), const),
            pl.BlockSpec((128, 1024), const),
        ] + _table_specs(),
        out_specs=(pl.BlockSpec((TM, D), lambda i: (i, 0)),) + specs,
        compiler_params=_params(("parallel",)),
        name="proj_odd",
    )(x, ctx, ya, yb, ybc, w_out, mod_prev, mod, ng, w, qg, kvg, wqb, wkvb, *tables)


def _flash_kernel(*refs, n_half, n_chunks, tk, tail):
    if n_half == 2:
        qt_ref, k_ref, vt_ref, g_ref, bl_ref, sg_ref, o_ref = refs
    else:
        qt_ref, k_ref, vt_ref, g_ref, o_ref = refs
    qt = qt_ref[...]
    tq = qt.shape[1]
    if n_half == 2:
        z = jnp.zeros((HEAD, tq), BF16)
        qs = (jnp.concatenate([qt[:HEAD], z], axis=0), jnp.concatenate([z, qt[HEAD:]], axis=0))
    else:
        qs = (qt,)

    def scores(r0, n):
        k = k_ref[pl.ds(r0, n), :]
        s0 = r0 // KV_SLAB
        vt = jnp.concatenate([vt_ref[s0 + j] for j in range(n // KV_SLAB)], axis=1)
        return [_dot(k, q) for q in qs], vt

    def accumulate(sts, vt, carry, exact_max):
        new = []
        for t in range(n_half):
            m_prev, l_prev, acc = carry[t]
            st = sts[t]
            if exact_max:
                m_new = jnp.maximum(m_prev, jnp.max(st, axis=0, keepdims=True))
                alpha = jnp.exp2(m_prev - m_new)
                pt = jnp.exp2(st - m_new)
                l_new = alpha * l_prev + jnp.sum(pt, axis=0, keepdims=True)
                acc = alpha * acc + _dot(vt, pt.astype(BF16))
            else:
                m_new = m_prev
                pt = jnp.exp2(st - m_prev)
                l_new = l_prev + jnp.sum(pt, axis=0, keepdims=True)
                acc = acc + _dot(vt, pt.astype(BF16))
            new.append((m_new, l_new, acc))
        return tuple(new)

    carry = tuple((jnp.full((1, tq), NEG, F32), jnp.zeros((1, tq), F32), jnp.zeros((LANES, tq), F32))
                  for _ in range(n_half))
    if tail:
        carry = accumulate(*scores(n_chunks * tk, tail), carry, True)
    if n_chunks == 1:
        carry = accumulate(*scores(0, tk), carry, not tail)
    elif n_chunks > 1:
        seeded = carry
        ahead = scores(0, tk)
        for c in range(n_chunks):
            cur, ahead = ahead, (scores((c + 1) * tk, tk) if c + 1 < n_chunks else None)
            carry = accumulate(*cur, carry, False)

        def redo(_):
            return lax.fori_loop(
                0, n_chunks, lambda c, cr: accumulate(*scores(pl.multiple_of(c * tk, tk), tk), cr, True), seeded)

        carry = lax.cond(_overflowed(*[x for cr in carry for x in cr[1:]]), redo, lambda cr: cr, carry)
    if n_half == 2:
        bl = bl_ref[...]
        lam = (jnp.exp(jnp.sum(bl[0:1] * bl[1:2], axis=1, keepdims=True))
               - jnp.exp(jnp.sum(bl[2:3] * bl[3:4], axis=1, keepdims=True)) + LAM_INIT)
        ot = carry[0][2] / carry[0][1] - lam * (carry[1][2] / carry[1][1])
        o = _rms(ot.T) * sg_ref[...] * (1.0 - LAM_INIT)
    else:
        o = (carry[0][2] / carry[0][1]).T
    o_ref[...] = (o * g_ref[...].astype(F32)).astype(BF16)


def _flash(qt, proj, vt, extra, *, n_half, dq, qrow, kcol, vrow, gcol, n_heads, q_row0, n_q_rows, kv_row0,
           n_kv_rows, tq, tk, name):
    nq, n_chunks = n_q_rows // tq, n_kv_rows // tk
    tail = n_kv_rows - n_chunks * tk
    qb0, kb0 = q_row0 // tq, kv_row0 // n_kv_rows
    in_specs = [
        pl.BlockSpec((dq, tq), lambda h, i: (qrow // dq + h, qb0 + i)),
        pl.BlockSpec((n_kv_rows, dq), lambda h, i: (kb0, kcol // dq + h)),
        pl.BlockSpec((n_kv_rows // KV_SLAB, LANES, KV_SLAB), lambda h, i: (kb0, vrow // LANES + h, 0)),
        pl.BlockSpec((tq, LANES), lambda h, i: (qb0 + i, gcol // LANES + h)),
    ]
    args = [qt, proj, vt, proj]
    if n_half == 2:
        in_specs += [pl.BlockSpec((4, HEAD), lambda h, i: (0, 0)),
                     pl.BlockSpec((1, LANES), lambda h, i: (0, 0))]
        args += list(extra)
    return pl.pallas_call(
        functools.partial(_flash_kernel, n_half=n_half, n_chunks=n_chunks, tk=tk, tail=tail),
        out_shape=jax.ShapeDtypeStruct((n_q_rows, n_heads * LANES), BF16),
        grid=(n_heads, nq),
        in_specs=in_specs,
        out_specs=pl.BlockSpec((tq, LANES), lambda h, i: (i, h)),
        compiler_params=_params(("parallel", "parallel")),
        name=name,
    )(*args)


def _zero_padded(qh, upper):
    z = jnp.zeros_like(qh)
    return jnp.concatenate([qh, z] if upper else [z, qh], axis=0)


def _store_head_pair(o_ref, g_ref, slab, top, bottom):
    o = jnp.concatenate([top, bottom], axis=0).T
    c = slice(slab * LANES, (slab + 1) * LANES)
    o_ref[:, c] = (o * g_ref[:, c].astype(F32)).astype(BF16)


def _local_softmax(st_c, st_l, floor, exact_max):
    m = jnp.max(st_c, axis=0, keepdims=True)
    if floor is not None:
        m = jnp.maximum(m, floor)
    if exact_max:
        m = jnp.maximum(m, jnp.max(st_l, axis=0, keepdims=True))
    pt_c = jnp.exp2(st_c - m)
    pt_l = jnp.exp2(st_l - m)
    denom = jnp.sum(pt_c, axis=0, keepdims=True) + jnp.sum(pt_l, axis=0, keepdims=True)
    if floor is not None:
        denom = denom + jnp.exp2(floor - m)
    return pt_c.astype(BF16), pt_l.astype(BF16), denom


def _local_pv(pt_c, pt_l, denom, vt_c, vt_l):
    pv = _dot(vt_c, pt_c) + _dot(vt_l, pt_l)
    return pv * (1.0 / denom), _overflowed(denom, pv)


def _streamed_then_exact(run):
    @pl.when(run(False))
    def _():
        run(True)


def _mixer_a_kernel(sink_ref, qt_ref, k0_ref, k1_ref, k2_ref, k3_ref, kc_ref, v0_ref, v1_ref, v2_ref, vc_ref,
                    g_ref, o_ref):
    i = pl.program_id(0)
    n_loc = TM + 2 * A_WINDOW
    k_l = jnp.concatenate([k0_ref[...], k1_ref[...], k2_ref[...], k3_ref[...]], axis=0)
    k_c = kc_ref[...]
    vt_l = jnp.concatenate([v0_ref[0][:, TM - A_WINDOW:], v1_ref[0], v2_ref[0][:, :A_WINDOW]], axis=1)
    vt_c = vc_ref[0]
    kl = lax.broadcasted_iota(jnp.int32, (n_loc, TM), 0)
    ql = lax.broadcasted_iota(jnp.int32, (n_loc, TM), 1)
    kpos = i * TM - A_WINDOW + kl
    dist = kl - ql
    ok = (jnp.where(dist >= 0, 1, 0) * jnp.where(dist <= 2 * A_WINDOW, 1, 0) * jnp.where(kpos >= 0, 1, 0)
          * jnp.where(kpos < S, 1, 0) * jnp.where(i < N_LAT_TILES, 1, 0))
    mask = jnp.where(ok > 0, 0.0, NEG)
    mask2 = jnp.concatenate([mask] * 2, axis=1)
    qt = qt_ref[...]

    def run(exact_max):
        pairs = range(4)
        ws = [jnp.concatenate([_zero_padded(qt[HEAD * hq:HEAD * (hq + 1)], p < 2) for hq in (2 * p, 2 * p + 1)],
                              axis=1) for p in pairs]
        snks = [jnp.concatenate([jnp.full((1, TM), sink_ref[hq] * LOG2E, F32) for hq in (2 * p, 2 * p + 1)],
                                axis=1) for p in pairs]
        st_cs = [_dot(k_c, w) for w in ws]
        st_ls = [_dot(k_l, w) + mask2 for w in ws]
        sms = [_local_softmax(st_c, st_l, snk, exact_max) for st_c, st_l, snk in zip(st_cs, st_ls, snks)]
        bad = None
        for p in pairs:
            ot, overflow = _local_pv(*sms[p], vt_c, vt_l)
            ot = ot[HEAD * (p // 2):HEAD * (p // 2 + 1)]
            _store_head_pair(o_ref, g_ref, p, ot[:, :TM], ot[:, TM:])
            bad = overflow if bad is None else jnp.logical_or(bad, overflow)
        return bad

    _streamed_then_exact(run)


def _mixer_a(qt, proj, vt, sink):
    nt = T // TM
    last_blk = T // A_WINDOW - 1

    def k_spec(off):
        return pl.BlockSpec((A_WINDOW, LANES),
                            lambda i: (jnp.clip(2 * i + off, 0, last_blk), E_KA // LANES))

    def v_spec(off):
        return pl.BlockSpec((1, LANES, KV_SLAB), lambda i: (jnp.clip(i + off, 0, nt - 1), EV_A // LANES, 0))

    return pl.pallas_call(
        _mixer_a_kernel,
        out_shape=jax.ShapeDtypeStruct((T, 512), BF16),
        grid=(nt,),
        in_specs=[pl.BlockSpec(memory_space=pltpu.SMEM),
                  pl.BlockSpec((512, TM), lambda i: (EQ_A // 512, i))]
        + [k_spec(off) for off in (-1, 0, 1, 2)]
        + [pl.BlockSpec((L, LANES), lambda i: (CTX_TILE, E_KA // LANES))]
        + [v_spec(off) for off in (-1, 0, 1)]
        + [pl.BlockSpec((1, LANES, KV_SLAB), lambda i: (CTX_TILE, EV_A // LANES, 0)),
           pl.BlockSpec((TM, 512), lambda i: (i, E_G // 512))],
        out_specs=pl.BlockSpec((TM, 512), lambda i: (i, 0)),
        compiler_params=_params(("parallel",)),
        name="mixer_a",
    )(sink, qt, *([proj] * 5), *([vt] * 4), proj)


def _mixer_d_kernel(qt_ref, k0_ref, k1_ref, k2_ref, kc_ref, v0_ref, v1_ref, v2_ref, vc_ref, bias_ref, g_ref, o_ref):
    k_l = jnp.concatenate([k0_ref[...], k1_ref[...], k2_ref[...]], axis=0)
    k_c = kc_ref[...]
    vt_l = jnp.concatenate([v0_ref[0], v1_ref[0], v2_ref[0]], axis=1)
    vt_c = vc_ref[0]
    qt = qt_ref[...]

    def run(exact_max):
        pairs = range(4)
        rows = [slice(pair * LANES, (pair + 1) * LANES) for pair in pairs]
        ws = [jnp.concatenate([_zero_padded(qt[r][:HEAD], True), _zero_padded(qt[r][HEAD:], False)], axis=1)
              for r in rows]
        st_cs = [_dot(k_c[:, r], w) for r, w in zip(rows, ws)]
        st_ls = [_dot(k_l[:, r], w) + jnp.concatenate([bias_ref[0, 2 * p], bias_ref[0, 2 * p + 1]], axis=1)
                 for p, r, w in zip(pairs, rows, ws)]
        sms = [_local_softmax(st_c, st_l, None, exact_max) for st_c, st_l in zip(st_cs, st_ls)]
        bad = None
        for p, r, sm in zip(pairs, rows, sms):
            ot, overflow = _local_pv(*sm, vt_c[r], vt_l[r])
            _store_head_pair(o_ref, g_ref, p, ot[:HEAD, :TM], ot[HEAD:, TM:])
            bad = overflow if bad is None else jnp.logical_or(bad, overflow)
        return bad

    _streamed_then_exact(run)


def _mixer_d(qt, proj, vt, bias):
    nt = S // TM

    def k_spec(off):
        return pl.BlockSpec((TM, 512), lambda i: (jnp.clip(i + off, 0, nt - 1), O_KD // 512))

    def v_spec(off):
        return pl.BlockSpec((1, 512, KV_SLAB), lambda i: (jnp.clip(i + off, 0, nt - 1), OV_D // 512, 0))

    def variant(i):
        return jnp.where(i == 0, 0, jnp.where(i == nt - 1, 2, 1))

    return pl.pallas_call(
        _mixer_d_kernel,
        out_shape=jax.ShapeDtypeStruct((S, 512), BF16),
        grid=(nt,),
        in_specs=[pl.BlockSpec((512, TM), lambda i: (OQ_D // 512, i))]
        + [k_spec(off) for off in (-1, 0, 1)]
        + [pl.BlockSpec((L, 512), lambda i: (CTX_TILE, O_KD // 512))]
        + [v_spec(off) for off in (-1, 0, 1)]
        + [pl.BlockSpec((1, 512, KV_SLAB), lambda i: (CTX_TILE, OV_D // 512, 0)),
           pl.BlockSpec((1, 8, 3 * TM, TM), lambda i: (variant(i), 0, 0, 0)),
           pl.BlockSpec((TM, 512), lambda i: (i, O_G // 512 + 1))],
        out_specs=pl.BlockSpec((TM, 512), lambda i: (i, 0)),
        compiler_params=_params(("parallel",)),
        name="mixer_d",
    )(qt, *([proj] * 4), *([vt] * 4), bias, proj)


D_ROWS, D_COLS = 8, 16
RPB_R, RPB_C = 2 * D_ROWS - 1, 2 * D_COLS - 1
Q_GRID_ROWS = Q_ROWS
K_GRID_ROWS = 3 * Q_GRID_ROWS


def _bias_kernel(rpb_ref, o_ref):
    base = pl.program_id(0) * (RPB_R * RPB_C)
    kc = lax.broadcasted_iota(jnp.int32, (GRID_W, LANES), 0)
    lane = lax.broadcasted_iota(jnp.int32, (GRID_W, LANES), 1)
    right = lane >= GRID_W
    qc = jnp.where(right, lane - GRID_W, lane)
    cs = jnp.clip(qc - D_COLS // 2, 0, GRID_W - D_COLS)
    col_ok = jnp.where(kc >= cs, 1, 0) * jnp.where(kc < cs + D_COLS, 1, 0)
    dc = jnp.where(col_ok > 0, kc - qc + (D_COLS - 1), -1)
    dc_l = jnp.where(right, -1, dc)
    dc_r = jnp.where(right, dc, -1)
    neg = jnp.full((GRID_W, LANES), NEG, F32)
    pairs = []
    for d in range(RPB_R - 1):
        t2 = neg
        for j in range(RPB_C):
            t2 = jnp.where(dc_l == j, rpb_ref[base + (d + 1) * RPB_C + j] * LOG2E, t2)
            t2 = jnp.where(dc_r == j, rpb_ref[base + d * RPB_C + j] * LOG2E, t2)
        pairs.append(t2)
    row_ok = (lambda ri, kr: kr >= Q_GRID_ROWS,
              lambda ri, kr: 0 <= kr - ri < D_ROWS,
              lambda ri, kr: kr < D_ROWS)
    for var in range(3):
        for kr in range(K_GRID_ROWS):
            for b in range(Q_GRID_ROWS // 2):
                blk = pairs[kr - 2 * b + 2]
                ok_l, ok_r = row_ok[var](2 * b, kr), row_ok[var](2 * b + 1, kr)
                if not ok_l and not ok_r:
                    blk = neg
                elif not ok_l:
                    blk = jnp.where(right, blk, NEG)
                elif not ok_r:
                    blk = jnp.where(right, NEG, blk)
                o_ref[var, 0, kr * GRID_W:(kr + 1) * GRID_W, b * LANES:(b + 1) * LANES] = blk


def _neighbourhood_bias(rpb):
    n_heads = rpb.shape[0]
    return pl.pallas_call(
        _bias_kernel,
        out_shape=jax.ShapeDtypeStruct((3, n_heads, 3 * TM, TM), F32),
        grid=(n_heads,),
        in_specs=[pl.BlockSpec(memory_space=pltpu.SMEM)],
        out_specs=pl.BlockSpec((3, 1, 3 * TM, TM), lambda h: (0, h, 0, 0)),
        compiler_params=_params(("parallel",)),
        name="nbr_bias",
    )(rpb.reshape(-1))


def _final_kernel(x_ref, y1_ref, y2_ref, w_ref, gate_ref, fg_ref, o_ref):
    y = jnp.concatenate([y1_ref[...], y2_ref[...]], axis=1)
    xn = x_ref[...] + gate_ref[0:1, :] * _dot(y, w_ref[...])
    o_ref[...] = _rms(xn) * fg_ref[...]


def _final_proj(x, y1, y2, w, mod, final_g):
    const = lambda i: (0, 0)
    return pl.pallas_call(
        _final_kernel,
        out_shape=jax.ShapeDtypeStruct((S, D), F32),
        grid=(S // OUT_TM,),
        in_specs=[
            pl.BlockSpec((OUT_TM, D), lambda i: (i, 0)),
            pl.BlockSpec((OUT_TM, 512), lambda i: (i, 0)),
            pl.BlockSpec((OUT_TM, 512), lambda i: (i, 0)),
            pl.BlockSpec((D, D), const),
            pl.BlockSpec((2, D), lambda i: (0, 2)),
            pl.BlockSpec((1, D), const),
        ],
        out_specs=pl.BlockSpec((OUT_TM, D), lambda i: (i, 0)),
        compiler_params=_params(("parallel",)),
        name="final_proj",
    )(x, y1, y2, w, mod, final_g)


def _rope_tables():
    quarter = HEAD // 4
    inv = ROPE_THETA ** (-jnp.arange(quarter, dtype=F32) / quarter)

    def parts(n):
        ang = jnp.arange(n).astype(F32)[:, None] * inv[None, :]
        cos, sin, z = jnp.cos(ang), jnp.sin(ang), jnp.zeros((n, quarter), F32)
        return (jnp.concatenate([cos, cos], axis=1), jnp.concatenate([-sin, z], axis=1),
                jnp.concatenate([z, sin], axis=1))

    def lanes(first32, second32):
        return jnp.tile(jnp.concatenate([first32, second32], axis=1), (1, LANES // HEAD))

    n_rows = S // GRID_W
    zr, zc = jnp.zeros((n_rows, 32), F32), jnp.zeros((GRID_W, 32), F32)
    ident = (jnp.ones((Q_ROWS, 32), F32), jnp.zeros((Q_ROWS, 32), F32), jnp.zeros((Q_ROWS, 32), F32))
    rt = jnp.concatenate([
        jnp.concatenate([lanes(p, zr).reshape(N_LAT_TILES, Q_ROWS, LANES) for p in parts(n_rows)], axis=1),
        jnp.concatenate([lanes(p, jnp.zeros((Q_ROWS, 32), F32)) for p in ident], axis=0)[None]], axis=0)
    ct_lat = jnp.stack([lanes(zc, p) for p in parts(GRID_W)])
    ct_ctx = jnp.stack([lanes(zc, jnp.full((GRID_W, 32), v, F32)) for v in (1.0, 0.0, 0.0)])
    return rt, jnp.stack([ct_lat, ct_ctx])


def kernel(x, c, ctx, c_ctx, ev_norm_g, ev_w_ada, ev_b_ada, ev_w_in, ev_a_sink, ev_b_lambda, ev_b_subln_g, ev_w_out, od_norm_g, od_w_ada, od_b_ada, od_w_in, od_c_q_norm_g, od_c_kv_norm_g, od_c_w_qb, od_c_w_kvb, od_d_rpb, od_w_out, final_norm_g):
    assert x.shape == (1, S, D) and ctx.shape == (1, L, D)
    assert ev_w_in.shape[0] == 1 and od_w_in.shape[0] == 1
    x0, xc0 = x[0], ctx[0]
    ct = jnp.stack([c[0], c_ctx], axis=1)
    tables = _rope_tables()
    qk_scale = HEAD ** -0.5

    mod = _ada(ct, ev_w_ada[0], ev_b_ada[0][None])
    w = ev_w_in[0]
    w_e = jnp.concatenate([w[:, :R_KA] * qk_scale, w[:, R_KA:R_QB], w[:, R_QB:R_KB] * qk_scale, w[:, R_KB:]],
                          axis=1).astype(BF16)
    proj, qt, vt = _proj_even(x0, xc0, mod, ev_norm_g[0][None], w_e, tables)
    ya = _mixer_a(qt, proj, vt, ev_a_sink[0])
    diff_extra = (ev_b_lambda[0], ev_b_subln_g[0][None])
    diff = functools.partial(_flash, qt, proj, vt, diff_extra, n_half=2, dq=128, qrow=EQ_B, kcol=E_KB, vrow=EV_B,
                             gcol=E_G + 512, n_heads=4)
    yb = diff(q_row0=0, n_q_rows=S, kv_row0=0, n_kv_rows=T, tq=FLASH_TQ, tk=FLASH_TK, name="diff_attn")
    ybc = diff(q_row0=S, n_q_rows=L, kv_row0=S, n_kv_rows=L, tq=L, tk=L, name="diff_attn_ctx")
    mod_prev = mod

    mod = _ada(ct, od_w_ada[0], od_b_ada[0][None])
    w = od_w_in[0]
    w_o = jnp.concatenate([w[:, 0:384], w[:, 384:448], w[:, 384:448], w[:, 448:960] * qk_scale, w[:, 960:]],
                          axis=1).astype(BF16)
    wqb = od_c_w_qb[0].reshape(256, 4, MLA_DK)
    wqb = jnp.concatenate([wqb[:, :, :128].reshape(256, 512), wqb[:, :, 128:].reshape(256, 256)],
                          axis=1).astype(BF16)
    x1, proj, qt, vt = _proj_odd(x0, xc0, ya, yb, ybc, ev_w_out[0].astype(BF16), mod_prev, mod, od_norm_g[0][None],
                                 w_o, od_c_q_norm_g[0][None], od_c_kv_norm_g[0][None], wqb,
                                 od_c_w_kvb[0].astype(BF16), tables)
    ym = _flash(qt, proj, vt, (), n_half=1, dq=256, qrow=OQ_M, kcol=O_KM, vrow=OV_M, gcol=O_G, n_heads=4,
                q_row0=0, n_q_rows=S, kv_row0=0, n_kv_rows=T, tq=FLASH_TQ, tk=FLASH_TK, name="mla_attn")
    yd = _mixer_d(qt, proj, vt, _neighbourhood_bias(od_d_rpb[0]))
    out = _final_proj(x1, ym, yd, od_w_out[0].astype(BF16), mod, final_norm_g[None])
    return out[None]
```
